```python
import jax, jax.numpy as jnp
from jax import lax
import numpy as np

D_MODEL = 2048
BATCH = 4
SEQ = 4096
DEPTH = 1

D_SSM = D_MODEL
SSM_HEADDIM = 64
SSM_HEADS = D_SSM // SSM_HEADDIM
SSM_GROUPS = 8
SSM_STATE = 128
SSM_CONV = 4
CHUNK = 128
DT_MIN = 1e-3
DT_MAX = 1e-1
D_CONV = D_MODEL
SHORT_CONV = 3
D_MIX = D_SSM + D_CONV
D_FF = -(-(8 * D_MODEL) // (3 * 256)) * 256
EPS = 1e-5

D_XBC = D_SSM + 2 * SSM_GROUPS * SSM_STATE
OFF_Z = 0
OFF_XBC = OFF_Z + D_SSM
OFF_DT = OFF_XBC + D_XBC
OFF_CB = OFF_DT + SSM_HEADS
OFF_CC = OFF_CB + D_CONV
OFF_CX = OFF_CC + D_CONV
D_IN = OFF_CX + D_CONV

kernel_name = "hymba_ssd_shortconv_block"


def _rmsnorm(x, g):
    xf = x.astype(jnp.float32)
    y = xf * lax.rsqrt(jnp.mean(xf * xf, axis=-1, keepdims=True) + EPS)
    return (y * g.astype(jnp.float32)).astype(x.dtype)


def _causal_dwconv(u, w):
    K = w.shape[0]
    S = u.shape[1]
    up = jnp.pad(u, ((0, 0), (K - 1, 0), (0, 0)))
    y = up[:, K - 1:K - 1 + S] * w[K - 1]
    for k in range(K - 1):
        y = y + up[:, k:k + S] * w[k]
    return y


def _ssd_chunked(xh, dt, A, Bm, Cm):
    b, S, H, P = xh.shape
    G, N = Bm.shape[2], Bm.shape[3]
    R = H // G
    nc = S // CHUNK
    f32 = jnp.float32
    X = (xh.astype(f32) * dt[..., None]).reshape(b, nc, CHUNK, G, R, P)
    dA = jnp.moveaxis((dt * A).reshape(b, nc, CHUNK, G, R), 2, -1)
    Bc = Bm.astype(f32).reshape(b, nc, CHUNK, G, N)
    Cc = Cm.astype(f32).reshape(b, nc, CHUNK, G, N)
    dA_cs = jnp.cumsum(dA, axis=-1)
    causal = jnp.tril(jnp.ones((CHUNK, CHUNK), dtype=bool))
    seg = dA_cs[..., :, None] - dA_cs[..., None, :]
    L = jnp.exp(jnp.where(causal, seg, -jnp.inf))
    CB = jnp.einsum('bclgn,bcsgn->bcgls', Cc, Bc)
    M = CB[:, :, :, None] * L
    y_diag = jnp.einsum('bcgrls,bcsgrp->bclgrp', M, X)
    decay_states = jnp.exp(dA_cs[..., -1:] - dA_cs)
    states = jnp.einsum('bclgn,bcgrl,bclgrp->bcgrpn', Bc, decay_states, X)
    chunk_decay = jnp.exp(dA_cs[..., -1])

    def step(h, inp):
        dec, st = inp
        return h * dec[..., None, None] + st, h

    h0 = jnp.zeros((b, G, R, P, N), f32)
    _, prev = lax.scan(step, h0, (jnp.moveaxis(chunk_decay, 1, 0), jnp.moveaxis(states, 1, 0)))
    prev = jnp.moveaxis(prev, 0, 1)
    y_off = jnp.einsum('bclgn,bcgrpn,bcgrl->bclgrp', Cc, prev, jnp.exp(dA_cs))
    return (y_diag + y_off).reshape(b, S, H, P)


def _ssd_group(z, xbc, dt_raw, conv_w, conv_b, dt_bias, A_log, Dskip, norm_g):
    b, S, _ = z.shape
    xbc = jax.nn.silu(_causal_dwconv(xbc, conv_w) + conv_b)
    xs = xbc[..., :D_SSM].reshape(b, S, SSM_HEADS, SSM_HEADDIM)
    Bm = xbc[..., D_SSM:D_SSM + SSM_GROUPS * SSM_STATE].reshape(b, S, SSM_GROUPS, SSM_STATE)
    Cm = xbc[..., D_SSM + SSM_GROUPS * SSM_STATE:].reshape(b, S, SSM_GROUPS, SSM_STATE)
    dt = jax.nn.softplus(dt_raw.astype(jnp.float32) + dt_bias.astype(jnp.float32))
    A = -jnp.exp(A_log.astype(jnp.float32))
    y = _ssd_chunked(xs, dt, A, Bm, Cm)
    y = y + Dskip.astype(jnp.float32)[:, None] * xs.astype(jnp.float32)
    y = y.reshape(b, S, D_SSM).astype(z.dtype)
    return _rmsnorm(y * jax.nn.silu(z), norm_g)


def _shortconv_group(gb, gc, u, conv_w):
    return gb * _causal_dwconv(gc * u, conv_w)


def setup_inputs(seed: int = 0) -> dict:
    key = jax.random.key(seed)
    ks = jax.random.split(key, 16)
    f32 = jnp.float32
    nrm = lambda k, shape, s: jax.random.normal(k, shape, f32) * s
    x = jax.random.normal(ks[0], (BATCH, SEQ, D_MODEL), f32)
    norm_mix_g = 1.0 + nrm(ks[1], (DEPTH, D_MODEL), 0.02)
    w_in = nrm(ks[2], (DEPTH, D_MODEL, D_IN), D_MODEL ** -0.5)
    ssm_conv_w = nrm(ks[3], (DEPTH, SSM_CONV, D_XBC), SSM_CONV ** -0.5)
    ssm_conv_b = nrm(ks[4], (DEPTH, D_XBC), 0.02)
    dt0 = jnp.exp(jax.random.uniform(ks[5], (DEPTH, SSM_HEADS), f32)
                  * (np.log(DT_MAX) - np.log(DT_MIN)) + np.log(DT_MIN))
    ssm_dt_bias = dt0 + jnp.log(-jnp.expm1(-dt0))
    ssm_A_log = jnp.log(jax.random.uniform(ks[6], (DEPTH, SSM_HEADS), f32, 1.0, 16.0))
    ssm_D = 1.0 + nrm(ks[7], (DEPTH, SSM_HEADS), 0.1)
    ssm_norm_g = 1.0 + nrm(ks[8], (DEPTH, D_SSM), 0.02)
    sc_conv_w = nrm(ks[9], (DEPTH, SHORT_CONV, D_CONV), SHORT_CONV ** -0.5)
    w_out = nrm(ks[10], (DEPTH, D_MIX, D_MODEL), D_MIX ** -0.5)
    norm_ffn_g = 1.0 + nrm(ks[11], (DEPTH, D_MODEL), 0.02)
    w_gate = nrm(ks[12], (DEPTH, D_MODEL, D_FF), D_MODEL ** -0.5)
    w_up = nrm(ks[13], (DEPTH, D_MODEL, D_FF), D_MODEL ** -0.5)
    w_down = nrm(ks[14], (DEPTH, D_FF, D_MODEL), D_FF ** -0.5)
    norm_final_g = 1.0 + nrm(ks[15], (D_MODEL,), 0.02)
    return {"x": x, "norm_mix_g": norm_mix_g, "w_in": w_in, "ssm_conv_w": ssm_conv_w,
            "ssm_conv_b": ssm_conv_b, "ssm_dt_bias": ssm_dt_bias, "ssm_A_log": ssm_A_log,
            "ssm_D": ssm_D, "ssm_norm_g": ssm_norm_g, "sc_conv_w": sc_conv_w, "w_out": w_out,
            "norm_ffn_g": norm_ffn_g, "w_gate": w_gate, "w_up": w_up, "w_down": w_down,
            "norm_final_g": norm_final_g}


def reference(x, norm_mix_g, w_in, ssm_conv_w, ssm_conv_b, ssm_dt_bias, ssm_A_log, ssm_D,
              ssm_norm_g, sc_conv_w, w_out, norm_ffn_g, w_gate, w_up, w_down, norm_final_g):
    h = x
    for l in range(DEPTH):
        n = _rmsnorm(h, norm_mix_g[l])
        proj = jnp.einsum('bsd,de->bse', n, w_in[l])
        y_ssm = _ssd_group(proj[..., OFF_Z:OFF_XBC], proj[..., OFF_XBC:OFF_DT],
                           proj[..., OFF_DT:OFF_CB], ssm_conv_w[l], ssm_conv_b[l],
                           ssm_dt_bias[l], ssm_A_log[l], ssm_D[l], ssm_norm_g[l])
        y_sc = _shortconv_group(proj[..., OFF_CB:OFF_CC], proj[..., OFF_CC:OFF_CX],
                                proj[..., OFF_CX:D_IN], sc_conv_w[l])
        y_mix = jnp.concatenate([y_ssm, y_sc], axis=-1)
        h = h + jnp.einsum('bse,ed->bsd', y_mix, w_out[l])
        n2 = _rmsnorm(h, norm_ffn_g[l])
        g = jnp.einsum('bsd,df->bsf', n2, w_gate[l])
        u = jnp.einsum('bsd,df->bsf', n2, w_up[l])
        h = h + jnp.einsum('bsf,fd->bsd', jax.nn.silu(g) * u, w_down[l])
    return _rmsnorm(h, norm_final_g)
```

```python
import functools

import jax
import jax.numpy as jnp
from jax import lax
from jax.experimental import pallas as pl
from jax.experimental.pallas import tpu as pltpu

F32 = jnp.float32
BF16 = jnp.bfloat16

EPS = 1e-5
SSM_HEADDIM = 64
SSM_GROUPS = 8
SSM_STATE = 128
SSM_CONV = 4
SHORT_CONV = 3
CHUNK = 128
HALO = 8
LANES = 128
VMEM_LIMIT = 60 * 1024 * 1024


def _cparams(sem):
    return pltpu.CompilerParams(dimension_semantics=sem, vmem_limit_bytes=VMEM_LIMIT)


def _rms_scale(x):
    return lax.rsqrt(jnp.mean(x * x, axis=-1, keepdims=True) + EPS)


def _silu(x):
    return x / (1.0 + jnp.exp(-x))


def _split3(v):
    hi = v.astype(BF16)
    r1 = v - hi.astype(F32)
    mid = r1.astype(BF16)
    lo = (r1 - mid.astype(F32)).astype(BF16)
    return hi, mid, lo


def _in_proj_kernel(x_ref, g_ref, w_ref, wdt_ref, out_ref, dt_ref, n_scr):
    @pl.when(pl.program_id(1) == 0)
    def _():
        x = x_ref[...]
        n = (x * _rms_scale(x) * g_ref[...]).astype(BF16)
        n_scr[...] = n
        dt_ref[...] = jnp.dot(n, wdt_ref[...], preferred_element_type=F32)

    out_ref[...] = jnp.dot(n_scr[...], w_ref[...], preferred_element_type=F32).astype(BF16)


def _in_proj(x, g, w_main, w_dt, tm, tn):
    T, D = x.shape
    N = w_main.shape[1]
    return pl.pallas_call(
        _in_proj_kernel,
        grid=(T // tm, N // tn),
        in_specs=[
            pl.BlockSpec((tm, D), lambda i, j: (i, 0)),
            pl.BlockSpec((1, D), lambda i, j: (0, 0)),
            pl.BlockSpec((D, tn), lambda i, j: (0, j)),
            pl.BlockSpec((D, LANES), lambda i, j: (0, 0)),
        ],
        out_specs=[
            pl.BlockSpec((tm, tn), lambda i, j: (i, j)),
            pl.BlockSpec((tm, LANES), lambda i, j: (i, 0)),
        ],
        out_shape=[
            jax.ShapeDtypeStruct((T, N), BF16),
            jax.ShapeDtypeStruct((T, LANES), F32),
        ],
        scratch_shapes=[pltpu.VMEM((tm, D), BF16)],
        compiler_params=_cparams(("parallel", "arbitrary")),
        name="in_proj",
    )(x, g, w_main, w_dt)


def _ssd_kernel(xbc_ref, z_ref, dt_ref, cw_ref, cb_ref, dtb_ref, alog_ref, dexp_ref, ng_ref,
                sel_ref, y_ref, buf, state, *, d_ssm):
    L = CHUNK
    G, N, P = SSM_GROUPS, SSM_STATE, SSM_HEADDIM
    R = d_ssm // (G * P)
    GW = R * P
    c = pl.program_id(1)

    @pl.when(c == 0)
    def _():
        buf[0:HALO, :] = jnp.zeros((HALO, buf.shape[1]), F32)
        state[...] = jnp.zeros(state.shape, F32)

    buf[HALO:HALO + L, :] = xbc_ref[...].astype(F32)
    conv = buf[HALO:HALO + L, :] * cw_ref[SSM_CONV - 1:SSM_CONV, :]
    for k in range(SSM_CONV - 1):
        off = HALO - (SSM_CONV - 1) + k
        conv = conv + buf[off:off + L, :] * cw_ref[k:k + 1, :]
    buf[0:HALO, :] = buf[L:L + HALO, :]
    xbc = _silu(conv + cb_ref[...])
    xs = xbc[:, :d_ssm]
    xs_b = xs.astype(BF16)
    Bm = xbc[:, d_ssm:d_ssm + G * N].astype(BF16)
    Cm = xbc[:, d_ssm + G * N:].astype(BF16)

    dt_in = dt_ref[...] + dtb_ref[...]
    dt = jnp.maximum(dt_in, 0.0) + jnp.log1p(jnp.exp(-jnp.abs(dt_in)))
    dA = dt * (-jnp.exp(alog_ref[...]))
    row = lax.broadcasted_iota(jnp.int32, (L, L), 0)
    col = lax.broadcasted_iota(jnp.int32, (L, L), 1)
    causal = row >= col
    tril = causal.astype(BF16)
    cs = jnp.dot(jnp.concatenate([tril, tril, tril], axis=1),
                 jnp.concatenate(_split3(dA), axis=0), preferred_element_type=F32)
    cs_last = cs[L - 1:L, :]
    ecs = jnp.exp(cs)
    wdec = dt * jnp.exp(cs_last - cs)
    cdec = jnp.exp(cs_last)
    stack = jnp.concatenate([wdec, ecs, jnp.broadcast_to(cdec, (HALO, LANES))], axis=0)
    expd = jnp.dot(jnp.concatenate(_split3(stack), axis=1), sel_ref[...],
                   preferred_element_type=F32)
    wdec_e = expd[0:L]
    ecs_e = expd[L:2 * L]
    cdec_e = expd[2 * L:2 * L + 1]
    tr = jnp.concatenate([cs[:, :32], dt[:, :32], jnp.zeros((L, LANES - 64), F32)], axis=1).T
    xt_b = (xs * wdec_e).astype(BF16)

    ys = []
    for g in range(G):
        Bg = Bm[:, g * N:(g + 1) * N]
        Cg = Cm[:, g * N:(g + 1) * N]
        CB = lax.dot_general(Cg, Bg, (((1,), (1,)), ((), ())), preferred_element_type=F32)
        Hs = state[g]
        y_off = jnp.dot(Cg, Hs.astype(BF16), preferred_element_type=F32)
        yd = []
        for r in range(R):
            h = g * R + r
            seg = cs[:, h:h + 1] - tr[h:h + 1, :]
            Lm = jnp.exp(jnp.where(causal, seg, -jnp.inf))
            M = (CB * Lm * tr[32 + h:33 + h, :]).astype(BF16)
            yd.append(jnp.dot(M, xs_b[:, h * P:(h + 1) * P], preferred_element_type=F32))
        y_diag = jnp.concatenate(yd, axis=1)
        sl = slice(g * GW, (g + 1) * GW)
        st_new = lax.dot_general(Bg, xt_b[:, sl], (((0,), (0,)), ((), ())),
                                 preferred_element_type=F32)
        state[g] = Hs * cdec_e[:, sl] + st_new
        ys.append(y_diag + y_off * ecs_e[:, sl] + dexp_ref[:, sl] * xs[:, sl])
    y = jnp.concatenate(ys, axis=1)
    yg = y * _silu(z_ref[...].astype(F32))
    y_ref[...] = (yg * _rms_scale(yg) * ng_ref[...]).astype(BF16)


def _ssd(proj, dt_raw, conv_w, conv_b, dt_bias, a_log, d_exp, norm_g, sel3, batch, seq, d_ssm):
    T = proj.shape[0]
    nc = seq // CHUNK
    d_xbc = conv_w.shape[1]
    G, N = SSM_GROUPS, SSM_STATE
    GW = d_ssm // G
    rowmap = lambda b, c: (b * nc + c, 0)
    const = lambda b, c: (0, 0)
    return pl.pallas_call(
        functools.partial(_ssd_kernel, d_ssm=d_ssm),
        grid=(batch, nc),
        in_specs=[
            pl.BlockSpec((CHUNK, d_xbc), rowmap),
            pl.BlockSpec((CHUNK, d_ssm), lambda b, c: (b * nc + c, d_xbc // d_ssm)),
            pl.BlockSpec((CHUNK, LANES), rowmap),
            pl.BlockSpec((SSM_CONV, d_xbc), const),
            pl.BlockSpec((1, d_xbc), const),
            pl.BlockSpec((1, LANES), const),
            pl.BlockSpec((1, LANES), const),
            pl.BlockSpec((1, d_ssm), const),
            pl.BlockSpec((1, d_ssm), const),
            pl.BlockSpec((3 * LANES, d_ssm), const),
        ],
        out_specs=pl.BlockSpec((CHUNK, d_ssm), rowmap),
        out_shape=jax.ShapeDtypeStruct((T, d_ssm), BF16),
        scratch_shapes=[pltpu.VMEM((CHUNK + HALO, d_xbc), F32),
                        pltpu.VMEM((G, N, GW), F32)],
        compiler_params=_cparams(("parallel", "arbitrary")),
        name="ssd",
    )(proj, proj, dt_raw, conv_w, conv_b, dt_bias, a_log, d_exp, norm_g, sel3)


def _shortconv_kernel(gb_ref, gc_ref, u_ref, w_ref, y_ref, buf):
    tm = gb_ref.shape[0]

    @pl.when(pl.program_id(1) == 0)
    def _():
        buf[0:HALO, :] = jnp.zeros((HALO, buf.shape[1]), F32)

    buf[HALO:HALO + tm, :] = gc_ref[...].astype(F32) * u_ref[...].astype(F32)
    conv = buf[HALO:HALO + tm, :] * w_ref[SHORT_CONV - 1:SHORT_CONV, :]
    for k in range(SHORT_CONV - 1):
        off = HALO - (SHORT_CONV - 1) + k
        conv = conv + buf[off:off + tm, :] * w_ref[k:k + 1, :]
    buf[0:HALO, :] = buf[tm:tm + HALO, :]
    y_ref[...] = (gb_ref[...].astype(F32) * conv).astype(BF16)


def _shortconv(proj, conv_w, batch, seq, col0, tm):
    T = proj.shape[0]
    C = conv_w.shape[1]
    nt = seq // tm
    blk = lambda k: pl.BlockSpec((tm, C), lambda b, t: (b * nt + t, col0 + k))
    return pl.pallas_call(
        _shortconv_kernel,
        grid=(batch, nt),
        in_specs=[blk(0), blk(1), blk(2), pl.BlockSpec((SHORT_CONV, C), lambda b, t: (0, 0))],
        out_specs=pl.BlockSpec((tm, C), lambda b, t: (b * nt + t, 0)),
        out_shape=jax.ShapeDtypeStruct((T, C), BF16),
        scratch_shapes=[pltpu.VMEM((tm + HALO, C), F32)],
        compiler_params=_cparams(("parallel", "arbitrary")),
        name="shortconv",
    )(proj, proj, proj, conv_w)


def _out_proj_kernel(ya_ref, yb_ref, wa_ref, wb_ref, x_ref, h_ref):
    acc = jnp.dot(ya_ref[...], wa_ref[...], preferred_element_type=F32)
    acc = acc + jnp.dot(yb_ref[...], wb_ref[...], preferred_element_type=F32)
    h_ref[...] = x_ref[...] + acc


def _out_proj(y_ssm, y_sc, w_out, x, tm, tn):
    T, D = x.shape
    Ka = y_ssm.shape[1]
    Kb = y_sc.shape[1]
    return pl.pallas_call(
        _out_proj_kernel,
        grid=(T // tm, D // tn),
        in_specs=[
            pl.BlockSpec((tm, Ka), lambda i, j: (i, 0)),
            pl.BlockSpec((tm, Kb), lambda i, j: (i, 0)),
            pl.BlockSpec((Ka, tn), lambda i, j: (0, j)),
            pl.BlockSpec((Kb, tn), lambda i, j: (Ka // Kb, j)),
            pl.BlockSpec((tm, tn), lambda i, j: (i, j)),
        ],
        out_specs=pl.BlockSpec((tm, tn), lambda i, j: (i, j)),
        out_shape=jax.ShapeDtypeStruct((T, D), F32),
        compiler_params=_cparams(("parallel", "arbitrary")),
        name="out_proj",
    )(y_ssm, y_sc, w_out, w_out, x)


def _ffn_kernel(h_ref, g_ref, wg_ref, wu_ref, wd_ref, gf_ref, o_ref, n_scr, *, final_norm, nsplit):
    f = pl.program_id(1)

    @pl.when(f == 0)
    def _():
        h = h_ref[...]
        n_scr[...] = (h * _rms_scale(h) * g_ref[...]).astype(BF16)
        o_ref[...] = h

    n = n_scr[...]
    gate = jnp.dot(n, wg_ref[...], preferred_element_type=F32)
    up = jnp.dot(n, wu_ref[...], preferred_element_type=F32)
    a = (_silu(gate) * up).astype(BF16)
    wn = o_ref.shape[1] // nsplit
    for s in range(nsplit):
        sl = slice(s * wn, (s + 1) * wn)
        o_ref[:, sl] += jnp.dot(a, wd_ref[:, sl], preferred_element_type=F32)

    if final_norm:
        @pl.when(f == pl.num_programs(1) - 1)
        def _():
            h2 = o_ref[...]
            o_ref[...] = h2 * _rms_scale(h2) * gf_ref[...]


def _ffn(h1, g, w_gate, w_up, w_down, g_final, final_norm, tm, tf):
    T, D = h1.shape
    F = w_gate.shape[1]
    return pl.pallas_call(
        functools.partial(_ffn_kernel, final_norm=final_norm, nsplit=4),
        grid=(T // tm, F // tf),
        in_specs=[
            pl.BlockSpec((tm, D), lambda i, f: (i, 0)),
            pl.BlockSpec((1, D), lambda i, f: (0, 0)),
            pl.BlockSpec((D, tf), lambda i, f: (0, f)),
            pl.BlockSpec((D, tf), lambda i, f: (0, f)),
            pl.BlockSpec((tf, D), lambda i, f: (f, 0)),
            pl.BlockSpec((1, D), lambda i, f: (0, 0)),
        ],
        out_specs=pl.BlockSpec((tm, D), lambda i, f: (i, 0)),
        out_shape=jax.ShapeDtypeStruct((T, D), F32),
        scratch_shapes=[pltpu.VMEM((tm, D), BF16)],
        compiler_params=_cparams(("parallel", "arbitrary")),
        name="ffn",
    )(h1, g, w_gate, w_up, w_down, g_final)


def _pad_lanes(v):
    return jnp.pad(v.reshape(1, -1), ((0, 0), (0, LANES - v.shape[-1])))


def kernel(x, norm_mix_g, w_in, ssm_conv_w, ssm_conv_b, ssm_dt_bias, ssm_A_log, ssm_D, ssm_norm_g,
           sc_conv_w, w_out, norm_ffn_g, w_gate, w_up, w_down, norm_final_g):
    batch, seq, d_model = x.shape
    depth = w_in.shape[0]
    d_ssm = ssm_norm_g.shape[1]
    d_xbc = ssm_conv_w.shape[2]
    heads = ssm_dt_bias.shape[1]
    d_conv = sc_conv_w.shape[2]
    assert d_ssm == d_conv == d_model and d_xbc == 2 * d_ssm and heads * SSM_HEADDIM == d_ssm
    assert heads <= 32 and seq % 512 == 0
    off_xbc = d_ssm
    off_dt = off_xbc + d_xbc
    off_cb = off_dt + heads

    sel = (jnp.arange(LANES)[:, None] == (jnp.arange(d_ssm) // SSM_HEADDIM)[None, :]).astype(BF16)
    sel3 = jnp.concatenate([sel, sel, sel], axis=0)

    h = x.reshape(batch * seq, d_model)
    for l in range(depth):
        w = w_in[l]
        w_main = jnp.concatenate([w[:, off_xbc:off_dt], w[:, :off_xbc], w[:, off_cb:]], axis=1).astype(BF16)
        w_dt = jnp.pad(w[:, off_dt:off_cb], ((0, 0), (0, LANES - heads))).astype(BF16)
        proj, dt_raw = _in_proj(h, norm_mix_g[l].reshape(1, -1), w_main, w_dt, tm=1024, tn=1024)
        y_ssm = _ssd(proj, dt_raw, ssm_conv_w[l], ssm_conv_b[l].reshape(1, -1), _pad_lanes(ssm_dt_bias[l]),
                     _pad_lanes(ssm_A_log[l]), jnp.repeat(ssm_D[l], SSM_HEADDIM).reshape(1, -1),
                     ssm_norm_g[l].reshape(1, -1), sel3, batch, seq, d_ssm)
        y_sc = _shortconv(proj, sc_conv_w[l], batch, seq, col0=(d_xbc + d_ssm) // d_conv, tm=512)
        h1 = _out_proj(y_ssm, y_sc, w_out[l].astype(BF16), h, tm=512, tn=1024)
        last = l == depth - 1
        h = _ffn(h1, norm_ffn_g[l].reshape(1, -1), w_gate[l].astype(BF16), w_up[l].astype(BF16),
                 w_down[l].astype(BF16), norm_final_g.reshape(1, -1), final_norm=last, tm=512, tf=512)
    return h.reshape(batch, seq, d_model)
```

```python
import functools

import jax
import jax.numpy as jnp
from jax import lax
from jax.experimental import pallas as pl
from jax.experimental.pallas import tpu as pltpu

F32 = jnp.float32
BF16 = jnp.bfloat16

EPS = 1e-5
SSM_HEADDIM = 64
SSM_GROUPS = 8
SSM_STATE = 128
SSM_CONV = 4
SHORT_CONV = 3
CHUNK = 128
HALO = 8
LANES = 128
VMEM_LIMIT = 60 * 1024 * 1024


def _cparams(sem):
    return pltpu.CompilerParams(dimension_semantics=sem, vmem_limit_bytes=VMEM_LIMIT)


def _rms_scale(x):
    return lax.rsqrt(jnp.mean(x * x, axis=-1, keepdims=True) + EPS)


def _silu(x):
    return x / (1.0 + jnp.exp(-x))


def _split3(v):
    hi = v.astype(BF16)
    r1 = v - hi.astype(F32)
    mid = r1.astype(BF16)
    lo = (r1 - mid.astype(F32)).astype(BF16)
    return hi, mid, lo


def _in_proj_kernel(x_ref, g_ref, wa_ref, wb_ref, wdt_ref, out_ref, dt_ref, n_scr, *, na):
    j = pl.program_id(1)

    @pl.when(j == 0)
    def _():
        x = x_ref[...]
        n = (x * _rms_scale(x) * g_ref[...]).astype(BF16)
        n_scr[...] = n
        dt_ref[...] = jnp.dot(n, wdt_ref[...], preferred_element_type=F32)

    @pl.when(j < na)
    def _():
        out_ref[...] = jnp.dot(n_scr[...], wa_ref[...], preferred_element_type=F32).astype(BF16)

    @pl.when(j >= na)
    def _():
        out_ref[...] = jnp.dot(n_scr[...], wb_ref[...], preferred_element_type=F32).astype(BF16)


def _in_proj(x, g, w_a, w_b, w_dt, tm, tn):
    T, D = x.shape
    na = w_a.shape[1] // tn
    nb = w_b.shape[1] // tn
    N = (na + nb) * tn
    return pl.pallas_call(
        functools.partial(_in_proj_kernel, na=na),
        grid=(T // tm, na + nb),
        in_specs=[
            pl.BlockSpec((tm, D), lambda i, j: (i, 0)),
            pl.BlockSpec((1, D), lambda i, j: (0, 0)),
            pl.BlockSpec((D, tn), lambda i, j: (0, jnp.minimum(j, na - 1))),
            pl.BlockSpec((D, tn), lambda i, j: (0, jnp.maximum(j - na, 0))),
            pl.BlockSpec((D, LANES), lambda i, j: (0, 0)),
        ],
        out_specs=[
            pl.BlockSpec((tm, tn), lambda i, j: (i, j)),
            pl.BlockSpec((tm, LANES), lambda i, j: (i, 0)),
        ],
        out_shape=[
            jax.ShapeDtypeStruct((T, N), BF16),
            jax.ShapeDtypeStruct((T, LANES), F32),
        ],
        scratch_shapes=[pltpu.VMEM((tm, D), BF16)],
        compiler_params=_cparams(("parallel", "arbitrary")),
        name="in_proj",
    )(x, g, w_a, w_b, w_dt)


def _ssd_kernel(z_ref, xs_ref, bc_ref, dt_ref, cw_ref, cb_ref, dtb_ref, alog_ref, dexp_ref, ng_ref,
                sel_ref, y_ref, buf, state, *, d_ssm):
    L = CHUNK
    G, N, P = SSM_GROUPS, SSM_STATE, SSM_HEADDIM
    R = d_ssm // (G * P)
    GW = R * P
    c = pl.program_id(1)

    @pl.when(c == 0)
    def _():
        buf[0:HALO, :] = jnp.zeros((HALO, buf.shape[1]), F32)
        state[...] = jnp.zeros(state.shape, F32)

    buf[HALO:HALO + L, :d_ssm] = xs_ref[...].astype(F32)
    buf[HALO:HALO + L, d_ssm:] = bc_ref[...].astype(F32)
    conv = buf[HALO:HALO + L, :] * cw_ref[SSM_CONV - 1:SSM_CONV, :]
    for k in range(SSM_CONV - 1):
        off = HALO - (SSM_CONV - 1) + k
        conv = conv + buf[off:off + L, :] * cw_ref[k:k + 1, :]
    buf[0:HALO, :] = buf[L:L + HALO, :]
    xbc = _silu(conv + cb_ref[...])
    xs = xbc[:, :d_ssm]
    xs_b = xs.astype(BF16)
    Bm = xbc[:, d_ssm:d_ssm + G * N].astype(BF16)
    Cm = xbc[:, d_ssm + G * N:].astype(BF16)

    dt_in = dt_ref[...] + dtb_ref[...]
    dt = jnp.maximum(dt_in, 0.0) + jnp.log1p(jnp.exp(-jnp.abs(dt_in)))
    dA = dt * (-jnp.exp(alog_ref[...]))
    row = lax.broadcasted_iota(jnp.int32, (L, L), 0)
    col = lax.broadcasted_iota(jnp.int32, (L, L), 1)
    causal = row >= col
    tril = causal.astype(BF16)
    cs = jnp.dot(jnp.concatenate([tril, tril, tril], axis=1),
                 jnp.concatenate(_split3(dA), axis=0), preferred_element_type=F32)
    cs_last = cs[L - 1:L, :]
    ecs = jnp.exp(cs)
    wdec = dt * jnp.exp(cs_last - cs)
    cdec = jnp.exp(cs_last)
    stack = jnp.concatenate([wdec, ecs, jnp.broadcast_to(cdec, (HALO, LANES))], axis=0)
    expd = jnp.dot(jnp.concatenate(_split3(stack), axis=1), sel_ref[...],
                   preferred_element_type=F32)
    wdec_e = expd[0:L]
    ecs_e = expd[L:2 * L]
    cdec_e = expd[2 * L:2 * L + 1]
    tr = jnp.concatenate([cs[:, :32], dt[:, :32], jnp.zeros((L, LANES - 64), F32)], axis=1).T
    xt_b = (xs * wdec_e).astype(BF16)

    ys = []
    for g in range(G):
        Bg = Bm[:, g * N:(g + 1) * N]
        Cg = Cm[:, g * N:(g + 1) * N]
        CB = lax.dot_general(Cg, Bg, (((1,), (1,)), ((), ())), preferred_element_type=F32)
        Hs = state[g]
        y_off = jnp.dot(Cg, Hs.astype(BF16), preferred_element_type=F32)
        yd = []
        for r in range(R):
            h = g * R + r
            seg = cs[:, h:h + 1] - tr[h:h + 1, :]
            Lm = jnp.exp(jnp.where(causal, seg, -jnp.inf))
            M = (CB * Lm * tr[32 + h:33 + h, :]).astype(BF16)
            yd.append(jnp.dot(M, xs_b[:, h * P:(h + 1) * P], preferred_element_type=F32))
        y_diag = jnp.concatenate(yd, axis=1)
        sl = slice(g * GW, (g + 1) * GW)
        st_new = lax.dot_general(Bg, xt_b[:, sl], (((0,), (0,)), ((), ())),
                                 preferred_element_type=F32)
        state[g] = Hs * cdec_e[:, sl] + st_new
        ys.append(y_diag + y_off * ecs_e[:, sl] + dexp_ref[:, sl] * xs[:, sl])
    y = jnp.concatenate(ys, axis=1)
    yg = y * _silu(z_ref[...].astype(F32))
    y_ref[...] = (yg * _rms_scale(yg) * ng_ref[...]).astype(BF16)


def _ssd(proj, dt_raw, conv_w, conv_b, dt_bias, a_log, d_exp, norm_g, sel3, batch, seq, d_ssm):
    T = proj.shape[0]
    nc = seq // CHUNK
    d_xbc = conv_w.shape[1]
    G, N = SSM_GROUPS, SSM_STATE
    GW = d_ssm // G
    rowmap = lambda b, c: (b * nc + c, 0)
    const = lambda b, c: (0, 0)
    return pl.pallas_call(
        functools.partial(_ssd_kernel, d_ssm=d_ssm),
        grid=(batch, nc),
        in_specs=[
            pl.BlockSpec((CHUNK, d_ssm), rowmap),
            pl.BlockSpec((CHUNK, d_ssm), lambda b, c: (b * nc + c, 1)),
            pl.BlockSpec((CHUNK, d_xbc - d_ssm), lambda b, c: (b * nc + c, 2)),
            pl.BlockSpec((CHUNK, LANES), rowmap),
            pl.BlockSpec((SSM_CONV, d_xbc), const),
            pl.BlockSpec((1, d_xbc), const),
            pl.BlockSpec((1, LANES), const),
            pl.BlockSpec((1, LANES), const),
            pl.BlockSpec((1, d_ssm), const),
            pl.BlockSpec((1, d_ssm), const),
            pl.BlockSpec((3 * LANES, d_ssm), const),
        ],
        out_specs=pl.BlockSpec((CHUNK, d_ssm), rowmap),
        out_shape=jax.ShapeDtypeStruct((T, d_ssm), BF16),
        scratch_shapes=[pltpu.VMEM((CHUNK + HALO, d_xbc), F32),
                        pltpu.VMEM((G, N, GW), F32)],
        compiler_params=_cparams(("parallel", "arbitrary")),
        name="ssd",
    )(proj, proj, proj, dt_raw, conv_w, conv_b, dt_bias, a_log, d_exp, norm_g, sel3)


def _shortconv_kernel(gb_ref, gc_ref, u_ref, w_ref, y_ref, buf):
    tm = gb_ref.shape[0]

    @pl.when(pl.program_id(1) == 0)
    def _():
        buf[0:HALO, :] = jnp.zeros((HALO, buf.shape[1]), F32)

    buf[HALO:HALO + tm, :] = gc_ref[...].astype(F32) * u_ref[...].astype(F32)
    conv = buf[HALO:HALO + tm, :] * w_ref[SHORT_CONV - 1:SHORT_CONV, :]
    for k in range(SHORT_CONV - 1):
        off = HALO - (SHORT_CONV - 1) + k
        conv = conv + buf[off:off + tm, :] * w_ref[k:k + 1, :]
    buf[0:HALO, :] = buf[tm:tm + HALO, :]
    y_ref[...] = (gb_ref[...].astype(F32) * conv).astype(BF16)


def _shortconv(proj, conv_w, batch, seq, col0, tm):
    T = proj.shape[0]
    C = conv_w.shape[1]
    nt = seq // tm
    blk = lambda k: pl.BlockSpec((tm, C), lambda b, t: (b * nt + t, col0 + k))
    return pl.pallas_call(
        _shortconv_kernel,
        grid=(batch, nt),
        in_specs=[blk(0), blk(1), blk(2), pl.BlockSpec((SHORT_CONV, C), lambda b, t: (0, 0))],
        out_specs=pl.BlockSpec((tm, C), lambda b, t: (b * nt + t, 0)),
        out_shape=jax.ShapeDtypeStruct((T, C), BF16),
        scratch_shapes=[pltpu.VMEM((tm + HALO, C), F32)],
        compiler_params=_cparams(("parallel", "arbitrary")),
        name="shortconv",
    )(proj, proj, proj, conv_w)


def _out_proj_kernel(ya_ref, yb_ref, w_ref, x_ref, h_ref, *, tn):
    ya = ya_ref[...]
    yb = yb_ref[...]
    Ka = ya.shape[1]
    for s in range(h_ref.shape[1] // tn):
        sl = slice(s * tn, (s + 1) * tn)
        acc = jnp.dot(ya, w_ref[:Ka, sl], preferred_element_type=F32)
        acc = acc + jnp.dot(yb, w_ref[Ka:, sl], preferred_element_type=F32)
        h_ref[:, sl] = x_ref[:, sl] + acc


def _out_proj(y_ssm, y_sc, w_out, x, tm, tn):
    T, D = x.shape
    Ka = y_ssm.shape[1]
    Kb = y_sc.shape[1]
    return pl.pallas_call(
        functools.partial(_out_proj_kernel, tn=tn),
        grid=(T // tm,),
        in_specs=[
            pl.BlockSpec((tm, Ka), lambda i: (i, 0)),
            pl.BlockSpec((tm, Kb), lambda i: (i, 0)),
            pl.BlockSpec((Ka + Kb, D), lambda i: (0, 0), pipeline_mode=pl.Buffered(1)),
            pl.BlockSpec((tm, D), lambda i: (i, 0)),
        ],
        out_specs=pl.BlockSpec((tm, D), lambda i: (i, 0)),
        out_shape=jax.ShapeDtypeStruct((T, D), F32),
        compiler_params=_cparams(("parallel",)),
        name="out_proj",
    )(y_ssm, y_sc, w_out, x)


def _ffn_kernel(h_ref, g_ref, wg_ref, wu_ref, wd_ref, gf_ref, o_ref, n_scr, *, final_norm, nsplit):
    f = pl.program_id(1)

    @pl.when(f == 0)
    def _():
        h = h_ref[...]
        n_scr[...] = (h * _rms_scale(h) * g_ref[...]).astype(BF16)
        o_ref[...] = h

    n = n_scr[...]
    gate = jnp.dot(n, wg_ref[...], preferred_element_type=F32)
    up = jnp.dot(n, wu_ref[...], preferred_element_type=F32)
    a = (_silu(gate) * up).astype(BF16)
    wn = o_ref.shape[1] // nsplit
    for s in range(nsplit):
        sl = slice(s * wn, (s + 1) * wn)
        o_ref[:, sl] += jnp.dot(a, wd_ref[:, sl], preferred_element_type=F32)

    if final_norm:
        @pl.when(f == pl.num_programs(1) - 1)
        def _():
            h2 = o_ref[...]
            o_ref[...] = h2 * _rms_scale(h2) * gf_ref[...]


def _ffn(h1, g, w_gate, w_up, w_down, g_final, final_norm, tm, tf):
    T, D = h1.shape
    F = w_gate.shape[1]
    return pl.pallas_call(
        functools.partial(_ffn_kernel, final_norm=final_norm, nsplit=4),
        grid=(T // tm, F // tf),
        in_specs=[
            pl.BlockSpec((tm, D), lambda i, f: (i, 0)),
            pl.BlockSpec((1, D), lambda i, f: (0, 0)),
            pl.BlockSpec((D, tf), lambda i, f: (0, f)),
            pl.BlockSpec((D, tf), lambda i, f: (0, f)),
            pl.BlockSpec((tf, D), lambda i, f: (f, 0)),
            pl.BlockSpec((1, D), lambda i, f: (0, 0)),
        ],
        out_specs=pl.BlockSpec((tm, D), lambda i, f: (i, 0)),
        out_shape=jax.ShapeDtypeStruct((T, D), F32),
        scratch_shapes=[pltpu.VMEM((tm, D), BF16)],
        compiler_params=_cparams(("parallel", "arbitrary")),
        name="ffn",
    )(h1, g, w_gate, w_up, w_down, g_final)


def _pad_lanes(v):
    return jnp.pad(v.reshape(1, -1), ((0, 0), (0, LANES - v.shape[-1])))


def kernel(x, norm_mix_g, w_in, ssm_conv_w, ssm_conv_b, ssm_dt_bias, ssm_A_log, ssm_D, ssm_norm_g,
           sc_conv_w, w_out, norm_ffn_g, w_gate, w_up, w_down, norm_final_g):
    batch, seq, d_model = x.shape
    depth = w_in.shape[0]
    d_ssm = ssm_norm_g.shape[1]
    d_xbc = ssm_conv_w.shape[2]
    heads = ssm_dt_bias.shape[1]
    d_conv = sc_conv_w.shape[2]
    assert d_ssm == d_conv == d_model and d_xbc == 2 * d_ssm and heads * SSM_HEADDIM == d_ssm
    assert heads <= 32 and seq % 512 == 0
    off_xbc = d_ssm
    off_dt = off_xbc + d_xbc
    off_cb = off_dt + heads

    sel = (jnp.arange(LANES)[:, None] == (jnp.arange(d_ssm) // SSM_HEADDIM)[None, :]).astype(BF16)
    sel3 = jnp.concatenate([sel, sel, sel], axis=0)

    h = x.reshape(batch * seq, d_model)
    for l in range(depth):
        w = w_in[l]
        w_a = w[:, :off_dt].astype(BF16)
        w_b = w[:, off_cb:].astype(BF16)
        w_dt = jnp.pad(w[:, off_dt:off_cb], ((0, 0), (0, LANES - heads))).astype(BF16)
        proj, dt_raw = _in_proj(h, norm_mix_g[l].reshape(1, -1), w_a, w_b, w_dt, tm=1024, tn=1024)
        y_ssm = _ssd(proj, dt_raw, ssm_conv_w[l], ssm_conv_b[l].reshape(1, -1), _pad_lanes(ssm_dt_bias[l]),
                     _pad_lanes(ssm_A_log[l]), jnp.repeat(ssm_D[l], SSM_HEADDIM).reshape(1, -1),
                     ssm_norm_g[l].reshape(1, -1), sel3, batch, seq, d_ssm)
        y_sc = _shortconv(proj, sc_conv_w[l], batch, seq, col0=(d_ssm + d_xbc) // d_conv, tm=512)
        h1 = _out_proj(y_ssm, y_sc, w_out[l].astype(BF16), h, tm=512, tn=512)
        last = l == depth - 1
        h = _ffn(h1, norm_ffn_g[l].reshape(1, -1), w_gate[l].astype(BF16), w_up[l].astype(BF16),
                 w_down[l].astype(BF16), norm_final_g.reshape(1, -1), final_norm=last, tm=1024, tf=512)
    return h.reshape(batch, seq, d_model)
```

```python
import functools

import jax
import jax.numpy as jnp
from jax import lax
from jax.experimental import pallas as pl
from jax.experimental.pallas import tpu as pltpu

F32 = jnp.float32
BF16 = jnp.bfloat16

EPS = 1e-5
SSM_HEADDIM = 64
SSM_GROUPS = 8
SSM_STATE = 128
SSM_CONV = 4
SHORT_CONV = 3
CHUNK = 128
SUBLANES = 8
LANES = 128
BF16_ROWS = 16
VMEM_LIMIT = 60 * 1024 * 1024


def _cparams(sem):
    return pltpu.CompilerParams(dimension_semantics=sem, vmem_limit_bytes=VMEM_LIMIT)


def _rms_scale(x):
    return lax.rsqrt(jnp.mean(x * x, axis=-1, keepdims=True) + EPS)


def _silu(x):
    h = 0.5 * x
    return h + h * jnp.tanh(h)


def _split3(v):
    hi = v.astype(BF16)
    r1 = v - hi.astype(F32)
    mid = r1.astype(BF16)
    lo = (r1 - mid.astype(F32)).astype(BF16)
    return hi, mid, lo


def _in_proj_kernel(*refs, na, nside):
    x_ref, g_ref, wa_ref, wb_ref, wdt_ref = refs[:5]
    side_in = refs[5:5 + nside]
    out_ref, dt_ref = refs[5 + nside:7 + nside]
    side_out = refs[7 + nside:7 + 2 * nside]
    n_scr = refs[7 + 2 * nside]
    j = pl.program_id(1)

    @pl.when(j == 0)
    def _():
        x = x_ref[...]
        n = (x * _rms_scale(x) * g_ref[...]).astype(BF16)
        n_scr[...] = n
        dt_ref[...] = jnp.dot(n, wdt_ref[...], preferred_element_type=F32)

    @pl.when(j < na)
    def _():
        out_ref[...] = jnp.dot(n_scr[...], wa_ref[...], preferred_element_type=F32).astype(BF16)

    @pl.when(j >= na)
    def _():
        out_ref[...] = jnp.dot(n_scr[...], wb_ref[...], preferred_element_type=F32).astype(BF16)

    for src, dst in zip(side_in, side_out):
        dst[...] = src[...].astype(BF16)


def _slab_rows(rows, steps):
    per = -(-rows // steps)
    per = -(-per // BF16_ROWS) * BF16_ROWS
    assert rows % per == 0, (rows, per)
    return per


def _in_proj(x, g, w_a, w_b, w_dt, side, tm, tn):
    T, D = x.shape
    na = w_a.shape[1] // tn
    nb = w_b.shape[1] // tn
    nj = na + nb
    N = nj * tn
    steps = (T // tm) * nj
    side_specs = []
    for w in side:
        rb = _slab_rows(w.shape[0], steps)
        last = w.shape[0] // rb - 1
        side_specs.append(pl.BlockSpec((rb, w.shape[1]),
                                       lambda i, j, last=last: (jnp.minimum(i * nj + j, last), 0)))
    outs = pl.pallas_call(
        functools.partial(_in_proj_kernel, na=na, nside=len(side)),
        grid=(T // tm, nj),
        in_specs=[
            pl.BlockSpec((tm, D), lambda i, j: (i, 0)),
            pl.BlockSpec((1, D), lambda i, j: (0, 0)),
            pl.BlockSpec((D, tn), lambda i, j: (0, jnp.minimum(j, na - 1))),
            pl.BlockSpec((D, tn), lambda i, j: (0, jnp.maximum(j - na, 0))),
            pl.BlockSpec((D, LANES), lambda i, j: (0, 0)),
        ] + side_specs,
        out_specs=[
            pl.BlockSpec((tm, tn), lambda i, j: (i, j)),
            pl.BlockSpec((tm, LANES), lambda i, j: (i, 0)),
        ] + side_specs,
        out_shape=[
            jax.ShapeDtypeStruct((T, N), BF16),
            jax.ShapeDtypeStruct((T, LANES), F32),
        ] + [jax.ShapeDtypeStruct(w.shape, BF16) for w in side],
        scratch_shapes=[pltpu.VMEM((tm, D), BF16)],
        compiler_params=_cparams(("arbitrary", "arbitrary")),
        name="in_proj",
    )(x, g, w_a, w_b, w_dt, *side)
    return outs[0], outs[1], outs[2:]


def _conv_even_odd(buf, c, w, width, rows):
    half = rows // 2
    shifted = {s: buf[c, pl.ds(SUBLANES - s, half, stride=2), :] for s in range(-1, width)}
    even = odd = None
    for k in range(width):
        tap = w[k:k + 1, :]
        e = tap * shifted[width - 1 - k]
        o = tap * shifted[width - 2 - k]
        even = e if even is None else even + e
        odd = o if odd is None else odd + o
    return even, odd


def _store_even_odd(dst, c, even, odd, rows):
    half = rows // 2
    dst[c, pl.ds(0, half, stride=2), :] = even
    dst[c, pl.ds(1, half, stride=2), :] = odd


def _mixer_kernel(z_ref, xs_ref, bc_ref, gb_ref, gc_ref, u_ref, dt_ref, cw_ref, cb_ref, scw_ref, dtb_ref,
                  alog_ref, dexp_ref, ng_ref, sel_ref, y_ref, ysc_ref, buf, act, pbuf, pact, state, *, heads):
    L = CHUNK
    G, N, P = SSM_GROUPS, SSM_STATE, SSM_HEADDIM
    R = heads // G
    GW = R * P
    nx = heads * P // LANES
    nsc = pbuf.shape[0]
    lane = lambda c: slice(c * LANES, (c + 1) * LANES)

    @pl.when(pl.program_id(1) == 0)
    def _():
        buf[:, 0:SUBLANES, :] = jnp.zeros((buf.shape[0], SUBLANES, LANES), F32)
        pbuf[:, 0:SUBLANES, :] = jnp.zeros((nsc, SUBLANES, LANES), F32)
        state[...] = jnp.zeros(state.shape, F32)

    for c in range(nsc):
        pbuf[c, SUBLANES:, :] = gc_ref[:, lane(c)].astype(F32) * u_ref[:, lane(c)].astype(F32)
    for c in range(nsc):
        even, odd = _conv_even_odd(pbuf, c, scw_ref[:, lane(c)], SHORT_CONV, L)
        _store_even_odd(pact, c, even, odd, L)
        pbuf[c, 0:SUBLANES, :] = pbuf[c, L:L + SUBLANES, :]
    for c in range(nsc):
        ysc_ref[:, lane(c)] = (gb_ref[:, lane(c)].astype(F32) * pact[c]).astype(BF16)

    for c in range(buf.shape[0]):
        src = xs_ref[:, lane(c)] if c < nx else bc_ref[:, lane(c - nx)]
        buf[c, SUBLANES:, :] = src.astype(F32)
    for c in range(buf.shape[0]):
        even, odd = _conv_even_odd(buf, c, cw_ref[:, lane(c)], SSM_CONV, L)
        b = cb_ref[:, lane(c)]
        _store_even_odd(act, c, _silu(even + b), _silu(odd + b), L)
        buf[c, 0:SUBLANES, :] = buf[c, L:L + SUBLANES, :]

    dt_in = dt_ref[...] + dtb_ref[...]
    dt = jnp.maximum(dt_in, 0.0) + jnp.log1p(jnp.exp(-jnp.abs(dt_in)))
    dA = dt * (-jnp.exp(alog_ref[...]))
    row = lax.broadcasted_iota(jnp.int32, (L, L), 0)
    col = lax.broadcasted_iota(jnp.int32, (L, L), 1)
    causal = row >= col
    tril = causal.astype(BF16)
    cs = jnp.dot(jnp.concatenate([tril, tril, tril], axis=1),
                 jnp.concatenate(_split3(dA), axis=0), preferred_element_type=F32)
    cs_last = cs[L - 1:L, :]
    ecs = jnp.exp(cs)
    wdec = dt * jnp.exp(cs_last - cs)
    cdec = jnp.exp(cs_last)
    stack = jnp.concatenate([wdec, ecs, jnp.broadcast_to(cdec, (SUBLANES, LANES))], axis=0)
    expd = jnp.dot(jnp.concatenate(_split3(stack), axis=1), sel_ref[...],
                   preferred_element_type=F32)
    wdec_e = expd[0:L]
    ecs_e = expd[L:2 * L]
    cdec_e = expd[2 * L:2 * L + 1]
    tr = jnp.concatenate([cs[:, :heads], dt[:, :heads], jnp.zeros((L, LANES - 2 * heads), F32)], axis=1).T

    ygs = []
    ssq = jnp.zeros((L, 1), F32)
    for g in range(G):
        sl = slice(g * GW, (g + 1) * GW)
        xs = jnp.concatenate([act[g * GW // LANES + k] for k in range(GW // LANES)], axis=1)
        xs_b = xs.astype(BF16)
        Bg = act[nx + g].astype(BF16)
        Cg = act[nx + G + g].astype(BF16)
        CB = lax.dot_general(Cg, Bg, (((1,), (1,)), ((), ())), preferred_element_type=F32)
        Hs = state[g]
        y_off = jnp.dot(Cg, Hs.astype(BF16), preferred_element_type=F32)
        yd = []
        for r in range(R):
            h = g * R + r
            seg = cs[:, h:h + 1] - tr[h:h + 1, :]
            Lm = jnp.exp(jnp.where(causal, seg, -jnp.inf))
            M = (CB * Lm * tr[heads + h:heads + h + 1, :]).astype(BF16)
            yd.append(jnp.dot(M, xs_b[:, r * P:(r + 1) * P], preferred_element_type=F32))
        y_diag = jnp.concatenate(yd, axis=1)
        xt_b = (xs * wdec_e[:, sl]).astype(BF16)
        st_new = lax.dot_general(Bg, xt_b, (((0,), (0,)), ((), ())), preferred_element_type=F32)
        state[g] = Hs * cdec_e[:, sl] + st_new
        y = y_diag + y_off * ecs_e[:, sl] + dexp_ref[:, sl] * xs
        yg = y * _silu(z_ref[:, sl].astype(F32))
        ssq = ssq + jnp.sum(yg * yg, axis=-1, keepdims=True)
        ygs.append(yg)
    scale = lax.rsqrt(ssq * (1.0 / (G * GW)) + EPS)
    for g in range(G):
        sl = slice(g * GW, (g + 1) * GW)
        y_ref[:, sl] = (ygs[g] * scale * ng_ref[:, sl]).astype(BF16)


def _mixer(proj, dt_raw, conv_w, conv_b, sc_conv_w, dt_bias, a_log, d_exp, norm_g, sel3, batch, seq, heads):
    T = proj.shape[0]
    nc = seq // CHUNK
    G, N = SSM_GROUPS, SSM_STATE
    d_ssm = heads * SSM_HEADDIM
    d_xbc = conv_w.shape[1]
    d_conv = sc_conv_w.shape[1]
    assert d_xbc - d_ssm == 2 * G * N == d_ssm == d_conv
    blk = lambda k: pl.BlockSpec((CHUNK, d_ssm), lambda b, c: (b * nc + c, k))
    const = lambda b, c: (0, 0)
    return pl.pallas_call(
        functools.partial(_mixer_kernel, heads=heads),
        grid=(batch, nc),
        in_specs=[
            blk(0), blk(1), blk(2), blk(3), blk(4), blk(5),
            pl.BlockSpec((CHUNK, LANES), lambda b, c: (b * nc + c, 0)),
            pl.BlockSpec((SSM_CONV, d_xbc), const),
            pl.BlockSpec((1, d_xbc), const),
            pl.BlockSpec((SHORT_CONV, d_conv), const),
            pl.BlockSpec((1, LANES), const),
            pl.BlockSpec((1, LANES), const),
            pl.BlockSpec((1, d_ssm), const),
            pl.BlockSpec((1, d_ssm), const),
            pl.BlockSpec((3 * LANES, d_ssm), const),
        ],
        out_specs=[blk(0), blk(0)],
        out_shape=[jax.ShapeDtypeStruct((T, d_ssm), BF16), jax.ShapeDtypeStruct((T, d_conv), BF16)],
        scratch_shapes=[pltpu.VMEM((d_xbc // LANES, SUBLANES + CHUNK, LANES), F32),
                        pltpu.VMEM((d_xbc // LANES, CHUNK, LANES), F32),
                        pltpu.VMEM((d_conv // LANES, SUBLANES + CHUNK, LANES), F32),
                        pltpu.VMEM((d_conv // LANES, CHUNK, LANES), F32),
                        pltpu.VMEM((G, N, d_ssm // G), F32)],
        compiler_params=_cparams(("parallel", "arbitrary")),
        name="mixer",
    )(proj, proj, proj, proj, proj, proj, dt_raw, conv_w, conv_b, sc_conv_w, dt_bias, a_log, d_exp, norm_g,
      sel3)


def _out_proj_kernel(ya_ref, yb_ref, w_ref, x_ref, h_ref, *, tn):
    ya = ya_ref[...]
    yb = yb_ref[...]
    Ka = ya.shape[1]
    for s in range(h_ref.shape[1] // tn):
        sl = slice(s * tn, (s + 1) * tn)
        acc = jnp.dot(ya, w_ref[:Ka, sl], preferred_element_type=F32)
        acc = acc + jnp.dot(yb, w_ref[Ka:, sl], preferred_element_type=F32)
        h_ref[:, sl] = x_ref[:, sl] + acc


def _out_proj(y_ssm, y_sc, w_out, x, tm, tn):
    T, D = x.shape
    Ka = y_ssm.shape[1]
    Kb = y_sc.shape[1]
    return pl.pallas_call(
        functools.partial(_out_proj_kernel, tn=tn),
        grid=(T // tm,),
        in_specs=[
            pl.BlockSpec((tm, Ka), lambda i: (i, 0)),
            pl.BlockSpec((tm, Kb), lambda i: (i, 0)),
            pl.BlockSpec((Ka + Kb, D), lambda i: (0, 0), pipeline_mode=pl.Buffered(1)),
            pl.BlockSpec((tm, D), lambda i: (i, 0)),
        ],
        out_specs=pl.BlockSpec((tm, D), lambda i: (i, 0)),
        out_shape=jax.ShapeDtypeStruct((T, D), F32),
        compiler_params=_cparams(("parallel",)),
        name="out_proj",
    )(y_ssm, y_sc, w_out, x)


def _ffn_kernel(h_ref, g_ref, wg_ref, wu_ref, wd_ref, gf_ref, o_ref, n_scr, *, final_norm, nsplit):
    f = pl.program_id(1)

    @pl.when(f == 0)
    def _():
        h = h_ref[...]
        n_scr[...] = (h * _rms_scale(h) * g_ref[...]).astype(BF16)
        o_ref[...] = h

    n = n_scr[...]
    gate = jnp.dot(n, wg_ref[...], preferred_element_type=F32)
    up = jnp.dot(n, wu_ref[...], preferred_element_type=F32)
    a = (_silu(gate) * up).astype(BF16)
    wn = o_ref.shape[1] // nsplit
    for s in range(nsplit):
        sl = slice(s * wn, (s + 1) * wn)
        o_ref[:, sl] += jnp.dot(a, wd_ref[:, sl], preferred_element_type=F32)

    if final_norm:
        @pl.when(f == pl.num_programs(1) - 1)
        def _():
            h2 = o_ref[...]
            o_ref[...] = h2 * _rms_scale(h2) * gf_ref[...]


def _ffn(h1, g, w_gate, w_up, w_down, g_final, final_norm, tm, tf):
    T, D = h1.shape
    F = w_gate.shape[1]
    return pl.pallas_call(
        functools.partial(_ffn_kernel, final_norm=final_norm, nsplit=4),
        grid=(T // tm, F // tf),
        in_specs=[
            pl.BlockSpec((tm, D), lambda i, f: (i, 0)),
            pl.BlockSpec((1, D), lambda i, f: (0, 0)),
            pl.BlockSpec((D, tf), lambda i, f: (0, f)),
            pl.BlockSpec((D, tf), lambda i, f: (0, f)),
            pl.BlockSpec((tf, D), lambda i, f: (f, 0)),
            pl.BlockSpec((1, D), lambda i, f: (0, 0)),
        ],
        out_specs=pl.BlockSpec((tm, D), lambda i, f: (i, 0)),
        out_shape=jax.ShapeDtypeStruct((T, D), F32),
        scratch_shapes=[pltpu.VMEM((tm, D), BF16)],
        compiler_params=_cparams(("parallel", "arbitrary")),
        name="ffn",
    )(h1, g, w_gate, w_up, w_down, g_final)


def _pad_lanes(v):
    return jnp.pad(v.reshape(1, -1), ((0, 0), (0, LANES - v.shape[-1])))


def kernel(x, norm_mix_g, w_in, ssm_conv_w, ssm_conv_b, ssm_dt_bias, ssm_A_log, ssm_D, ssm_norm_g,
           sc_conv_w, w_out, norm_ffn_g, w_gate, w_up, w_down, norm_final_g):
    batch, seq, d_model = x.shape
    depth = w_in.shape[0]
    d_ssm = ssm_norm_g.shape[1]
    d_xbc = ssm_conv_w.shape[2]
    heads = ssm_dt_bias.shape[1]
    d_conv = sc_conv_w.shape[2]
    assert d_ssm == d_conv == d_model and heads * SSM_HEADDIM == d_ssm and 2 * heads <= LANES
    assert seq % 1024 == 0
    off_xbc = d_ssm
    off_dt = off_xbc + d_xbc
    off_cb = off_dt + heads

    sel = (jnp.arange(LANES)[:, None] == (jnp.arange(d_ssm) // SSM_HEADDIM)[None, :]).astype(BF16)
    sel3 = jnp.concatenate([sel, sel, sel], axis=0)

    h = x.reshape(batch * seq, d_model)
    for l in range(depth):
        w = w_in[l]
        w_a = w[:, :off_dt].astype(BF16)
        w_b = w[:, off_cb:].astype(BF16)
        w_dt = jnp.pad(w[:, off_dt:off_cb], ((0, 0), (0, LANES - heads))).astype(BF16)
        proj, dt_raw, (wo_b, wg_b, wu_b, wd_b) = _in_proj(
            h, norm_mix_g[l].reshape(1, -1), w_a, w_b, w_dt, (w_out[l], w_gate[l], w_up[l], w_down[l]),
            tm=1024, tn=1024)
        y_ssm, y_sc = _mixer(proj, dt_raw, ssm_conv_w[l], ssm_conv_b[l].reshape(1, -1), sc_conv_w[l],
                             _pad_lanes(ssm_dt_bias[l]), _pad_lanes(ssm_A_log[l]),
                             jnp.repeat(ssm_D[l], SSM_HEADDIM).reshape(1, -1), ssm_norm_g[l].reshape(1, -1),
                             sel3, batch, seq, heads)
        h1 = _out_proj(y_ssm, y_sc, wo_b, h, tm=512, tn=512)
        last = l == depth - 1
        h = _ffn(h1, norm_ffn_g[l].reshape(1, -1), wg_b, wu_b, wd_b, norm_final_g.reshape(1, -1),
                 final_norm=last, tm=1024, tf=512)
    return h.reshape(batch, seq, d_model)
```

```python
import functools

import jax
import jax.numpy as jnp
from jax import lax
from jax.experimental import pallas as pl
from jax.experimental.pallas import tpu as pltpu

F32 = jnp.float32
BF16 = jnp.bfloat16

EPS = 1e-5
LOG2E = 1.4426950408889634
SSM_HEADDIM = 64
SSM_GROUPS = 8
SSM_STATE = 128
SSM_CONV = 4
SHORT_CONV = 3
CHUNK = 128
SUBLANES = 8
LANES = 128
BF16_ROWS = 16
VMEM_LIMIT = 60 * 1024 * 1024


def _cparams(sem):
    return pltpu.CompilerParams(dimension_semantics=sem, vmem_limit_bytes=VMEM_LIMIT)


def _rms_scale(x):
    return lax.rsqrt(jnp.mean(x * x, axis=-1, keepdims=True) + EPS)


def _silu(x):
    h = 0.5 * x
    return h + h * jnp.tanh(h)


def _split3(v):
    hi = v.astype(BF16)
    r1 = v - hi.astype(F32)
    mid = r1.astype(BF16)
    lo = (r1 - mid.astype(F32)).astype(BF16)
    return hi, mid, lo


def _prep_kernel(a_ref, b_ref, c_ref, wa_ref, wb_ref, wdt_ref, *, shift):
    wa_ref[...] = a_ref[...].astype(BF16)
    tn = b_ref.shape[0]
    wb_ref[:tn - shift, :] = b_ref[shift:, :].astype(BF16)
    wb_ref[tn - shift:, :] = c_ref[...].astype(BF16)

    @pl.when(pl.program_id(0) == 0)
    def _():
        wdt_ref[:shift, :] = b_ref[:shift, :].astype(BF16)
        wdt_ref[shift:, :] = jnp.zeros((LANES - shift, wdt_ref.shape[1]), BF16)


def _prep_w_in(wt, off_b, shift, tn):
    D = wt.shape[1]
    nt = off_b // tn
    assert off_b % tn == 0 and tn % shift == 0 and shift % BF16_ROWS == 0 and wt.shape[0] == 2 * off_b + shift
    return pl.pallas_call(
        functools.partial(_prep_kernel, shift=shift),
        grid=(nt,),
        in_specs=[
            pl.BlockSpec((tn, D), lambda k: (k, 0)),
            pl.BlockSpec((tn, D), lambda k: (nt + k, 0)),
            pl.BlockSpec((shift, D), lambda k: ((nt + k + 1) * (tn // shift), 0)),
        ],
        out_specs=[
            pl.BlockSpec((tn, D), lambda k: (k, 0)),
            pl.BlockSpec((tn, D), lambda k: (k, 0)),
            pl.BlockSpec((LANES, D), lambda k: (0, 0)),
        ],
        out_shape=[jax.ShapeDtypeStruct((off_b, D), BF16), jax.ShapeDtypeStruct((off_b, D), BF16),
                   jax.ShapeDtypeStruct((LANES, D), BF16)],
        compiler_params=_cparams(("arbitrary",)),
        name="prep_w_in",
    )(wt, wt, wt)


def _in_proj_kernel(*refs, na, nside):
    x_ref, g_ref, wa_ref, wb_ref, wdt_ref = refs[:5]
    side_in = refs[5:5 + nside]
    out_ref, dt_ref = refs[5 + nside:7 + nside]
    side_out = refs[7 + nside:7 + 2 * nside]
    n_scr = refs[7 + 2 * nside]
    j = pl.program_id(1)

    def mm(lhs, wt_ref):
        return lax.dot_general(lhs, wt_ref[...], (((1,), (1,)), ((), ())), preferred_element_type=F32)

    @pl.when(j == 0)
    def _():
        x = x_ref[...]
        n = (x * _rms_scale(x) * g_ref[...]).astype(BF16)
        n_scr[...] = n
        dt_ref[...] = mm(n, wdt_ref)

    @pl.when(j < na)
    def _():
        out_ref[...] = mm(n_scr[...], wa_ref).astype(BF16)

    @pl.when(j >= na)
    def _():
        out_ref[...] = mm(n_scr[...], wb_ref).astype(BF16)

    for src, dst in zip(side_in, side_out):
        dst[...] = src[...].astype(BF16)


def _slab_rows(rows, steps):
    per = -(-rows // steps)
    per = -(-per // BF16_ROWS) * BF16_ROWS
    assert rows % per == 0, (rows, per)
    return per


def _in_proj(x, g, w_a, w_b, w_dt, side, tm, tn):
    T, D = x.shape
    na = w_a.shape[0] // tn
    nb = w_b.shape[0] // tn
    nj = na + nb
    N = nj * tn
    steps = (T // tm) * nj
    side_specs = []
    for w in side:
        rb = _slab_rows(w.shape[0], steps)
        last = w.shape[0] // rb - 1
        side_specs.append(pl.BlockSpec((rb, w.shape[1]),
                                       lambda i, j, last=last: (jnp.minimum(i * nj + j, last), 0)))
    outs = pl.pallas_call(
        functools.partial(_in_proj_kernel, na=na, nside=len(side)),
        grid=(T // tm, nj),
        in_specs=[
            pl.BlockSpec((tm, D), lambda i, j: (i, 0)),
            pl.BlockSpec((1, D), lambda i, j: (0, 0)),
            pl.BlockSpec((tn, D), lambda i, j: (jnp.minimum(j, na - 1), 0)),
            pl.BlockSpec((tn, D), lambda i, j: (jnp.maximum(j - na, 0), 0)),
            pl.BlockSpec((LANES, D), lambda i, j: (0, 0)),
        ] + side_specs,
        out_specs=[
            pl.BlockSpec((tm, tn), lambda i, j: (i, j)),
            pl.BlockSpec((tm, LANES), lambda i, j: (i, 0)),
        ] + side_specs,
        out_shape=[
            jax.ShapeDtypeStruct((T, N), BF16),
            jax.ShapeDtypeStruct((T, LANES), F32),
        ] + [jax.ShapeDtypeStruct(w.shape, BF16) for w in side],
        scratch_shapes=[pltpu.VMEM((tm, D), BF16)],
        compiler_params=_cparams(("arbitrary", "arbitrary")),
        name="in_proj",
    )(x, g, w_a, w_b, w_dt, *side)
    return outs[0], outs[1], outs[2:]


def _conv_even_odd(buf, c, w, width, rows):
    half = rows // 2
    shifted = {s: buf[c, pl.ds(SUBLANES - s, half, stride=2), :] for s in range(-1, width)}
    even = odd = None
    for k in range(width):
        tap = w[k:k + 1, :]
        e = tap * shifted[width - 1 - k]
        o = tap * shifted[width - 2 - k]
        even = e if even is None else even + e
        odd = o if odd is None else odd + o
    return even, odd


def _store_even_odd(dst, c, even, odd, rows):
    half = rows // 2
    dst[c, pl.ds(0, half, stride=2), :] = even
    dst[c, pl.ds(1, half, stride=2), :] = odd


def _mixer_kernel(z_ref, xs_ref, bc_ref, gb_ref, gc_ref, u_ref, dt_ref, cw_ref, cb_ref, scw_ref, dtb_ref,
                  alog_ref, dexp_ref, ng_ref, sel_ref, y_ref, ysc_ref, buf, act, pbuf, pact, state, *, heads):
    L = CHUNK
    G, N, P = SSM_GROUPS, SSM_STATE, SSM_HEADDIM
    R = heads // G
    GW = R * P
    nx = heads * P // LANES
    nsc = pbuf.shape[0]
    lane = lambda c: slice(c * LANES, (c + 1) * LANES)

    @pl.when(pl.program_id(1) == 0)
    def _():
        buf[:, 0:SUBLANES, :] = jnp.zeros((buf.shape[0], SUBLANES, LANES), F32)
        pbuf[:, 0:SUBLANES, :] = jnp.zeros((nsc, SUBLANES, LANES), F32)
        state[...] = jnp.zeros(state.shape, F32)

    dt_in = (dt_ref[...] + dtb_ref[...]).T[:heads]
    dt = jnp.maximum(dt_in, 0.0) + jnp.log1p(jnp.exp(-jnp.abs(dt_in)))
    a2 = jnp.broadcast_to(-LOG2E * jnp.exp(alog_ref[...]), (L, LANES)).T[:heads]
    dA2 = dt * a2
    row = lax.broadcasted_iota(jnp.int32, (L, L), 0)
    col = lax.broadcasted_iota(jnp.int32, (L, L), 1)
    causal = row >= col
    triu = (row <= col).astype(BF16)
    cs2 = jnp.dot(jnp.concatenate(_split3(dA2), axis=1), jnp.concatenate([triu, triu, triu], axis=0),
                  preferred_element_type=F32)
    cs2_last = cs2[:, L - 1:L]
    ecs = jnp.exp2(cs2)
    wdec = dt * jnp.exp2(cs2_last - cs2)
    csd = cs2 - jnp.log2(dt)
    colT = jnp.concatenate([cs2, wdec, ecs, jnp.zeros((LANES - 3 * heads, L), F32)], axis=0).T
    expd = jnp.dot(jnp.concatenate(_split3(colT), axis=1), sel_ref[...],
                   preferred_element_type=F32)
    wdec_e = expd[:, :heads * P]
    ecs_e = expd[:, heads * P:]
    cdec_e = ecs_e[L - 1:L, :]

    for c in range(buf.shape[0]):
        src = xs_ref[:, lane(c)] if c < nx else bc_ref[:, lane(c - nx)]
        buf[c, SUBLANES:, :] = src.astype(F32)
    for c in range(buf.shape[0]):
        even, odd = _conv_even_odd(buf, c, cw_ref[:, lane(c)], SSM_CONV, L)
        b = cb_ref[:, lane(c)]
        _store_even_odd(act, c, _silu(even + b), _silu(odd + b), L)
        buf[c, 0:SUBLANES, :] = buf[c, L:L + SUBLANES, :]

    ygs = []
    ssq = jnp.zeros((L, 1), F32)
    for g in range(G):
        sl = slice(g * GW, (g + 1) * GW)
        xs = jnp.concatenate([act[g * GW // LANES + k] for k in range(GW // LANES)], axis=1)
        xs_b = xs.astype(BF16)
        Bg = act[nx + g].astype(BF16)
        Cg = act[nx + G + g].astype(BF16)
        CB = lax.dot_general(Cg, Bg, (((1,), (1,)), ((), ())), preferred_element_type=F32)
        Hs = state[g]
        y_off = jnp.dot(Cg, Hs.astype(BF16), preferred_element_type=F32)
        yd = []
        for r in range(R):
            h = g * R + r
            seg = colT[:, h:h + 1] - csd[h:h + 1, :]
            M = (CB * jnp.exp2(jnp.where(causal, seg, -jnp.inf))).astype(BF16)
            yd.append(jnp.dot(M, xs_b[:, r * P:(r + 1) * P], preferred_element_type=F32))
        y_diag = jnp.concatenate(yd, axis=1)
        xt_b = (xs * wdec_e[:, sl]).astype(BF16)
        st_new = lax.dot_general(Bg, xt_b, (((0,), (0,)), ((), ())), preferred_element_type=F32)
        state[g] = Hs * cdec_e[:, sl] + st_new
        y = y_diag + y_off * ecs_e[:, sl] + dexp_ref[:, sl] * xs
        yg = y * _silu(z_ref[:, sl].astype(F32))
        ssq = ssq + jnp.sum(yg * yg, axis=-1, keepdims=True)
        ygs.append(yg)
    scale = lax.rsqrt(ssq * (1.0 / (G * GW)) + EPS)
    for g in range(G):
        sl = slice(g * GW, (g + 1) * GW)
        y_ref[:, sl] = (ygs[g] * scale * ng_ref[:, sl]).astype(BF16)

    for c in range(nsc):
        pbuf[c, SUBLANES:, :] = gc_ref[:, lane(c)].astype(F32) * u_ref[:, lane(c)].astype(F32)
    for c in range(nsc):
        even, odd = _conv_even_odd(pbuf, c, scw_ref[:, lane(c)], SHORT_CONV, L)
        _store_even_odd(pact, c, even, odd, L)
        pbuf[c, 0:SUBLANES, :] = pbuf[c, L:L + SUBLANES, :]
    for c in range(nsc):
        ysc_ref[:, lane(c)] = (gb_ref[:, lane(c)].astype(F32) * pact[c]).astype(BF16)


def _mixer(proj, dt_raw, conv_w, conv_b, sc_conv_w, dt_bias, a_log, d_exp, norm_g, sel3, batch, seq, heads):
    T = proj.shape[0]
    nc = seq // CHUNK
    G, N = SSM_GROUPS, SSM_STATE
    d_ssm = heads * SSM_HEADDIM
    d_xbc = conv_w.shape[1]
    d_conv = sc_conv_w.shape[1]
    assert d_xbc - d_ssm == 2 * G * N == d_ssm == d_conv
    blk = lambda k: pl.BlockSpec((CHUNK, d_ssm), lambda b, c: (b * nc + c, k))
    const = lambda b, c: (0, 0)
    return pl.pallas_call(
        functools.partial(_mixer_kernel, heads=heads),
        grid=(batch, nc),
        in_specs=[
            blk(0), blk(1), blk(2), blk(3), blk(4), blk(5),
            pl.BlockSpec((CHUNK, LANES), lambda b, c: (b * nc + c, 0)),
            pl.BlockSpec((SSM_CONV, d_xbc), const),
            pl.BlockSpec((1, d_xbc), const),
            pl.BlockSpec((SHORT_CONV, d_conv), const),
            pl.BlockSpec((1, LANES), const),
            pl.BlockSpec((1, LANES), const),
            pl.BlockSpec((1, d_ssm), const),
            pl.BlockSpec((1, d_ssm), const),
            pl.BlockSpec((3 * LANES, 2 * d_ssm), const),
        ],
        out_specs=[blk(0), blk(0)],
        out_shape=[jax.ShapeDtypeStruct((T, d_ssm), BF16), jax.ShapeDtypeStruct((T, d_conv), BF16)],
        scratch_shapes=[pltpu.VMEM((d_xbc // LANES, SUBLANES + CHUNK, LANES), F32),
                        pltpu.VMEM((d_xbc // LANES, CHUNK, LANES), F32),
                        pltpu.VMEM((d_conv // LANES, SUBLANES + CHUNK, LANES), F32),
                        pltpu.VMEM((d_conv // LANES, CHUNK, LANES), F32),
                        pltpu.VMEM((G, N, d_ssm // G), F32)],
        compiler_params=_cparams(("parallel", "arbitrary")),
        name="mixer",
    )(proj, proj, proj, proj, proj, proj, dt_raw, conv_w, conv_b, sc_conv_w, dt_bias, a_log, d_exp, norm_g,
      sel3)


def _out_proj_kernel(ya_ref, yb_ref, w_ref, x_ref, h_ref, *, tn):
    ya = ya_ref[...]
    yb = yb_ref[...]
    Ka = ya.shape[1]
    for s in range(h_ref.shape[1] // tn):
        sl = slice(s * tn, (s + 1) * tn)
        acc = jnp.dot(ya, w_ref[:Ka, sl], preferred_element_type=F32)
        acc = acc + jnp.dot(yb, w_ref[Ka:, sl], preferred_element_type=F32)
        h_ref[:, sl] = x_ref[:, sl] + acc


def _out_proj(y_ssm, y_sc, w_out, x, tm, tn):
    T, D = x.shape
    Ka = y_ssm.shape[1]
    Kb = y_sc.shape[1]
    return pl.pallas_call(
        functools.partial(_out_proj_kernel, tn=tn),
        grid=(T // tm,),
        in_specs=[
            pl.BlockSpec((tm, Ka), lambda i: (i, 0)),
            pl.BlockSpec((tm, Kb), lambda i: (i, 0)),
            pl.BlockSpec((Ka + Kb, D), lambda i: (0, 0), pipeline_mode=pl.Buffered(1)),
            pl.BlockSpec((tm, D), lambda i: (i, 0)),
        ],
        out_specs=pl.BlockSpec((tm, D), lambda i: (i, 0)),
        out_shape=jax.ShapeDtypeStruct((T, D), F32),
        compiler_params=_cparams(("parallel",)),
        name="out_proj",
    )(y_ssm, y_sc, w_out, x)


def _ffn_kernel(h_ref, g_ref, wg_ref, wu_ref, wd_ref, gf_ref, o_ref, n_scr, *, final_norm, nsplit):
    f = pl.program_id(1)

    @pl.when(f == 0)
    def _():
        h = h_ref[...]
        n_scr[...] = (h * _rms_scale(h) * g_ref[...]).astype(BF16)
        o_ref[...] = h

    n = n_scr[...]
    gate = jnp.dot(n, wg_ref[...], preferred_element_type=F32)
    up = jnp.dot(n, wu_ref[...], preferred_element_type=F32)
    a = (_silu(gate) * up).astype(BF16)
    wn = o_ref.shape[1] // nsplit
    for s in range(nsplit):
        sl = slice(s * wn, (s + 1) * wn)
        o_ref[:, sl] += jnp.dot(a, wd_ref[:, sl], preferred_element_type=F32)

    if final_norm:
        @pl.when(f == pl.num_programs(1) - 1)
        def _():
            h2 = o_ref[...]
            o_ref[...] = h2 * _rms_scale(h2) * gf_ref[...]


def _ffn(h1, g, w_gate, w_up, w_down, g_final, final_norm, tm, tf):
    T, D = h1.shape
    F = w_gate.shape[1]
    return pl.pallas_call(
        functools.partial(_ffn_kernel, final_norm=final_norm, nsplit=4),
        grid=(T // tm, F // tf),
        in_specs=[
            pl.BlockSpec((tm, D), lambda i, f: (i, 0)),
            pl.BlockSpec((1, D), lambda i, f: (0, 0)),
            pl.BlockSpec((D, tf), lambda i, f: (0, f)),
            pl.BlockSpec((D, tf), lambda i, f: (0, f)),
            pl.BlockSpec((tf, D), lambda i, f: (f, 0)),
            pl.BlockSpec((1, D), lambda i, f: (0, 0)),
        ],
        out_specs=pl.BlockSpec((tm, D), lambda i, f: (i, 0)),
        out_shape=jax.ShapeDtypeStruct((T, D), F32),
        scratch_shapes=[pltpu.VMEM((tm, D), BF16)],
        compiler_params=_cparams(("parallel", "arbitrary")),
        name="ffn",
    )(h1, g, w_gate, w_up, w_down, g_final)


def _pad_lanes(v):
    return jnp.pad(v.reshape(1, -1), ((0, 0), (0, LANES - v.shape[-1])))


def kernel(x, norm_mix_g, w_in, ssm_conv_w, ssm_conv_b, ssm_dt_bias, ssm_A_log, ssm_D, ssm_norm_g,
           sc_conv_w, w_out, norm_ffn_g, w_gate, w_up, w_down, norm_final_g):
    batch, seq, d_model = x.shape
    depth = w_in.shape[0]
    d_ssm = ssm_norm_g.shape[1]
    d_xbc = ssm_conv_w.shape[2]
    heads = ssm_dt_bias.shape[1]
    d_conv = sc_conv_w.shape[2]
    assert d_ssm == d_conv == d_model and heads * SSM_HEADDIM == d_ssm and 3 * heads <= LANES
    assert seq % 1024 == 0
    off_xbc = d_ssm
    off_dt = off_xbc + d_xbc
    off_cb = off_dt + heads

    head_of_col = (jnp.arange(d_ssm) // SSM_HEADDIM)[None, :]
    lane_id = jnp.arange(LANES)[:, None]
    sel = jnp.concatenate([lane_id == heads + head_of_col, lane_id == 2 * heads + head_of_col], axis=1)
    sel3 = jnp.concatenate([sel, sel, sel], axis=0).astype(BF16)

    h = x.reshape(batch * seq, d_model)
    for l in range(depth):
        assert w_in.shape[2] - off_cb == off_dt
        w_a, w_b, w_dt = _prep_w_in(w_in[l].T, off_dt, heads, tn=512)
        proj, dt_raw, (wo_b, wg_b, wu_b, wd_b) = _in_proj(
            h, norm_mix_g[l].reshape(1, -1), w_a, w_b, w_dt, (w_out[l], w_gate[l], w_up[l], w_down[l]),
            tm=1024, tn=1024)
        y_ssm, y_sc = _mixer(proj, dt_raw, ssm_conv_w[l], ssm_conv_b[l].reshape(1, -1), sc_conv_w[l],
                             _pad_lanes(ssm_dt_bias[l]), _pad_lanes(ssm_A_log[l]),
                             jnp.repeat(ssm_D[l], SSM_HEADDIM).reshape(1, -1), ssm_norm_g[l].reshape(1, -1),
                             sel3, batch, seq, heads)
        h1 = _out_proj(y_ssm, y_sc, wo_b, h, tm=512, tn=512)
        last = l == depth - 1
        h = _ffn(h1, norm_ffn_g[l].reshape(1, -1), wg_b, wu_b, wd_b, norm_final_g.reshape(1, -1),
                 final_norm=last, tm=1024, tf=512)
    return h.reshape(batch, seq, d_model)
```

```python
import functools

import jax
import jax.numpy as jnp
from jax import lax
from jax.experimental import pallas as pl
from jax.experimental.pallas import tpu as pltpu

F32 = jnp.float32
BF16 = jnp.bfloat16

EPS = 1e-5
LOG2E = 1.4426950408889634
SSM_HEADDIM = 64
SSM_GROUPS = 8
SSM_STATE = 128
SSM_CONV = 4
SHORT_CONV = 3
CHUNK = 128
SUBLANES = 8
LANES = 128
BF16_ROWS = 16
VMEM_LIMIT = 60 * 1024 * 1024


def _cparams(sem):
    return pltpu.CompilerParams(dimension_semantics=sem, vmem_limit_bytes=VMEM_LIMIT)


def _rms_scale(x):
    return lax.rsqrt(jnp.mean(x * x, axis=-1, keepdims=True) + EPS)


def _silu(x):
    h = 0.5 * x
    return h + h * jnp.tanh(h)


def _split3(v):
    hi = v.astype(BF16)
    r1 = v - hi.astype(F32)
    mid = r1.astype(BF16)
    lo = (r1 - mid.astype(F32)).astype(BF16)
    return hi, mid, lo


def _prep_kernel(blk_ref, nxt_ref, w_ref, wdt_ref, *, nt, shift):
    k = pl.program_id(0)
    tn = blk_ref.shape[0]

    @pl.when(k < nt)
    def _():
        w_ref[...] = blk_ref[...].astype(BF16)

    @pl.when(k >= nt)
    def _():
        w_ref[:tn - shift, :] = blk_ref[shift:, :].astype(BF16)
        w_ref[tn - shift:, :] = nxt_ref[...].astype(BF16)

    @pl.when(k == nt)
    def _():
        wdt_ref[:shift, :] = blk_ref[:shift, :].astype(BF16)
        wdt_ref[shift:, :] = jnp.zeros((LANES - shift, wdt_ref.shape[1]), BF16)


def _prep_w_in(wt, off_b, shift, tn):
    D = wt.shape[1]
    nt = off_b // tn
    assert off_b % tn == 0 and tn % shift == 0 and shift % BF16_ROWS == 0 and wt.shape[0] == 2 * off_b + shift
    return pl.pallas_call(
        functools.partial(_prep_kernel, nt=nt, shift=shift),
        grid=(2 * nt,),
        in_specs=[
            pl.BlockSpec((tn, D), lambda k: (k, 0)),
            pl.BlockSpec((shift, D), lambda k: ((k + 1) * (tn // shift), 0)),
        ],
        out_specs=[
            pl.BlockSpec((tn, D), lambda k: (k, 0)),
            pl.BlockSpec((LANES, D), lambda k: (0, 0)),
        ],
        out_shape=[jax.ShapeDtypeStruct((2 * off_b, D), BF16), jax.ShapeDtypeStruct((LANES, D), BF16)],
        compiler_params=_cparams(("arbitrary",)),
        name="prep_w_in",
    )(wt, wt)


def _in_proj_kernel(*refs, nside):
    x_ref, g_ref, w_ref, wdt_ref = refs[:4]
    side_in = refs[4:4 + nside]
    out_ref, dt_ref = refs[4 + nside:6 + nside]
    side_out = refs[6 + nside:6 + 2 * nside]
    n_scr = refs[6 + 2 * nside]

    def mm(lhs, wt_ref):
        return lax.dot_general(lhs, wt_ref[...], (((1,), (1,)), ((), ())), preferred_element_type=F32)

    @pl.when(pl.program_id(1) == 0)
    def _():
        x = x_ref[...]
        n = (x * _rms_scale(x) * g_ref[...]).astype(BF16)
        n_scr[...] = n
        dt_ref[...] = mm(n, wdt_ref)

    out_ref[...] = mm(n_scr[...], w_ref).astype(BF16)
    for src, dst in zip(side_in, side_out):
        dst[...] = src[...].astype(BF16)


def _slab_rows(rows, steps):
    per = -(-rows // steps)
    per = -(-per // BF16_ROWS) * BF16_ROWS
    assert rows % per == 0, (rows, per)
    return per


def _in_proj(x, g, w, w_dt, side, tm, tn):
    T, D = x.shape
    N = w.shape[0]
    nj = N // tn
    steps = (T // tm) * nj
    side_specs = []
    for s in side:
        rb = _slab_rows(s.shape[0], steps)
        last = s.shape[0] // rb - 1
        side_specs.append(pl.BlockSpec((rb, s.shape[1]),
                                       lambda i, j, last=last: (jnp.minimum(i * nj + j, last), 0)))
    outs = pl.pallas_call(
        functools.partial(_in_proj_kernel, nside=len(side)),
        grid=(T // tm, nj),
        in_specs=[
            pl.BlockSpec((tm, D), lambda i, j: (i, 0)),
            pl.BlockSpec((1, D), lambda i, j: (0, 0)),
            pl.BlockSpec((tn, D), lambda i, j: (j, 0)),
            pl.BlockSpec((LANES, D), lambda i, j: (0, 0)),
        ] + side_specs,
        out_specs=[
            pl.BlockSpec((tm, tn), lambda i, j: (i, j)),
            pl.BlockSpec((tm, LANES), lambda i, j: (i, 0)),
        ] + side_specs,
        out_shape=[
            jax.ShapeDtypeStruct((T, N), BF16),
            jax.ShapeDtypeStruct((T, LANES), F32),
        ] + [jax.ShapeDtypeStruct(s.shape, BF16) for s in side],
        scratch_shapes=[pltpu.VMEM((tm, D), BF16)],
        compiler_params=_cparams(("arbitrary", "arbitrary")),
        name="in_proj",
    )(x, g, w, w_dt, *side)
    return outs[0], outs[1], outs[2:]


def _conv_even_odd(buf, c, w, width, rows):
    half = rows // 2
    shifted = {s: buf[c, pl.ds(SUBLANES - s, half, stride=2), :] for s in range(-1, width)}
    even = odd = None
    for k in range(width):
        tap = w[k:k + 1, :]
        e = tap * shifted[width - 1 - k]
        o = tap * shifted[width - 2 - k]
        even = e if even is None else even + e
        odd = o if odd is None else odd + o
    return even, odd


def _store_even_odd(dst, c, even, odd, rows):
    half = rows // 2
    dst[c, pl.ds(0, half, stride=2), :] = even
    dst[c, pl.ds(1, half, stride=2), :] = odd


def _mixer_kernel(z_ref, xs_ref, bc_ref, gb_ref, gc_ref, u_ref, dt_ref, cw_ref, cb_ref, scw_ref, dtb_ref,
                  alog_ref, dexp_ref, ng_ref, y_ref, ysc_ref, buf, act, pbuf, pact, state, sel, *, heads):
    L = CHUNK
    G, N, P = SSM_GROUPS, SSM_STATE, SSM_HEADDIM
    R = heads // G
    GW = R * P
    nx = heads * P // LANES
    nsc = pbuf.shape[0]
    lane = lambda c: slice(c * LANES, (c + 1) * LANES)

    @pl.when(pl.program_id(1) == 0)
    def _():
        buf[:, 0:SUBLANES, :] = jnp.zeros((buf.shape[0], SUBLANES, LANES), F32)
        pbuf[:, 0:SUBLANES, :] = jnp.zeros((nsc, SUBLANES, LANES), F32)
        state[...] = jnp.zeros(state.shape, F32)
        src_lane = lax.broadcasted_iota(jnp.int32, (LANES, 2 * heads * P), 0)
        out_col = lax.broadcasted_iota(jnp.int32, (LANES, 2 * heads * P), 1)
        one = (src_lane == heads + out_col // P).astype(BF16)
        for k in range(3):
            sel[k * LANES:(k + 1) * LANES, :] = one

    def chunk(ci, carry):
        rows = pl.ds(pl.multiple_of(ci * L, L), L)

        dt_in = (dt_ref[rows, :] + dtb_ref[...]).T[:heads]
        dt = jnp.maximum(dt_in, 0.0) + jnp.log1p(jnp.exp(-jnp.abs(dt_in)))
        a2 = jnp.broadcast_to(-LOG2E * jnp.exp(alog_ref[...]), (L, LANES)).T[:heads]
        dA2 = dt * a2
        row = lax.broadcasted_iota(jnp.int32, (L, L), 0)
        col = lax.broadcasted_iota(jnp.int32, (L, L), 1)
        causal = row >= col
        triu = (row <= col).astype(BF16)
        cs2 = jnp.dot(jnp.concatenate(_split3(dA2), axis=1), jnp.concatenate([triu, triu, triu], axis=0),
                      preferred_element_type=F32)
        cs2_last = cs2[:, L - 1:L]
        ecs = jnp.exp2(cs2)
        wdec = dt * jnp.exp2(cs2_last - cs2)
        csd = cs2 - jnp.log2(dt)
        colT = jnp.concatenate([cs2, wdec, ecs, jnp.zeros((LANES - 3 * heads, L), F32)], axis=0).T
        expd = jnp.dot(jnp.concatenate(_split3(colT), axis=1), sel[...],
                       preferred_element_type=F32)
        wdec_e = expd[:, :heads * P]
        ecs_e = expd[:, heads * P:]
        cdec_e = ecs_e[L - 1:L, :]

        for c in range(buf.shape[0]):
            src = xs_ref[rows, lane(c)] if c < nx else bc_ref[rows, lane(c - nx)]
            buf[c, SUBLANES:, :] = src.astype(F32)
        for c in range(buf.shape[0]):
            even, odd = _conv_even_odd(buf, c, cw_ref[:, lane(c)], SSM_CONV, L)
            b = cb_ref[:, lane(c)]
            _store_even_odd(act, c, _silu(even + b), _silu(odd + b), L)
            buf[c, 0:SUBLANES, :] = buf[c, L:L + SUBLANES, :]

        ygs = []
        ssq = jnp.zeros((L, 1), F32)
        for g in range(G):
            sl = slice(g * GW, (g + 1) * GW)
            xs = jnp.concatenate([act[g * GW // LANES + k] for k in range(GW // LANES)], axis=1)
            xs_b = xs.astype(BF16)
            Bg = act[nx + g].astype(BF16)
            Cg = act[nx + G + g].astype(BF16)
            CB = lax.dot_general(Cg, Bg, (((1,), (1,)), ((), ())), preferred_element_type=F32)
            Hs = state[g]
            y_off = jnp.dot(Cg, Hs.astype(BF16), preferred_element_type=F32)
            yd = []
            for r in range(R):
                h = g * R + r
                seg = colT[:, h:h + 1] - csd[h:h + 1, :]
                M = (CB * jnp.exp2(jnp.where(causal, seg, -jnp.inf))).astype(BF16)
                yd.append(jnp.dot(M, xs_b[:, r * P:(r + 1) * P], preferred_element_type=F32))
            y_diag = jnp.concatenate(yd, axis=1)
            xt_b = (xs * wdec_e[:, sl]).astype(BF16)
            st_new = lax.dot_general(Bg, xt_b, (((0,), (0,)), ((), ())), preferred_element_type=F32)
            state[g] = Hs * cdec_e[:, sl] + st_new
            y = y_diag + y_off * ecs_e[:, sl] + dexp_ref[:, sl] * xs
            yg = y * _silu(z_ref[rows, sl].astype(F32))
            ssq = ssq + jnp.sum(yg * yg, axis=-1, keepdims=True)
            ygs.append(yg)
        scale = lax.rsqrt(ssq * (1.0 / (G * GW)) + EPS)
        for g in range(G):
            sl = slice(g * GW, (g + 1) * GW)
            y_ref[rows, sl] = (ygs[g] * scale * ng_ref[:, sl]).astype(BF16)

        for c in range(nsc):
            pbuf[c, SUBLANES:, :] = gc_ref[rows, lane(c)].astype(F32) * u_ref[rows, lane(c)].astype(F32)
        for c in range(nsc):
            even, odd = _conv_even_odd(pbuf, c, scw_ref[:, lane(c)], SHORT_CONV, L)
            _store_even_odd(pact, c, even, odd, L)
            pbuf[c, 0:SUBLANES, :] = pbuf[c, L:L + SUBLANES, :]
        for c in range(nsc):
            ysc_ref[rows, lane(c)] = (gb_ref[rows, lane(c)].astype(F32) * pact[c]).astype(BF16)
        return carry

    lax.fori_loop(0, z_ref.shape[0] // L, chunk, 0)


def _mixer(proj, dt_raw, conv_w, conv_b, sc_conv_w, dt_bias, a_log, d_exp, norm_g, batch, seq, heads, tm):
    T = proj.shape[0]
    nt = seq // tm
    G, N = SSM_GROUPS, SSM_STATE
    d_ssm = heads * SSM_HEADDIM
    d_xbc = conv_w.shape[1]
    d_conv = sc_conv_w.shape[1]
    assert d_xbc - d_ssm == 2 * G * N == d_ssm == d_conv and tm % CHUNK == 0 and seq % tm == 0
    blk = lambda k: pl.BlockSpec((tm, d_ssm), lambda b, t: (b * nt + t, k))
    const = lambda b, t: (0, 0)
    return pl.pallas_call(
        functools.partial(_mixer_kernel, heads=heads),
        grid=(batch, nt),
        in_specs=[
            blk(0), blk(1), blk(2), blk(3), blk(4), blk(5),
            pl.BlockSpec((tm, LANES), lambda b, t: (b * nt + t, 0)),
            pl.BlockSpec((SSM_CONV, d_xbc), const),
            pl.BlockSpec((1, d_xbc), const),
            pl.BlockSpec((SHORT_CONV, d_conv), const),
            pl.BlockSpec((1, LANES), const),
            pl.BlockSpec((1, LANES), const),
            pl.BlockSpec((1, d_ssm), const),
            pl.BlockSpec((1, d_ssm), const),
        ],
        out_specs=[blk(0), blk(0)],
        out_shape=[jax.ShapeDtypeStruct((T, d_ssm), BF16), jax.ShapeDtypeStruct((T, d_conv), BF16)],
        scratch_shapes=[pltpu.VMEM((d_xbc // LANES, SUBLANES + CHUNK, LANES), F32),
                        pltpu.VMEM((d_xbc // LANES, CHUNK, LANES), F32),
                        pltpu.VMEM((d_conv // LANES, SUBLANES + CHUNK, LANES), F32),
                        pltpu.VMEM((d_conv // LANES, CHUNK, LANES), F32),
                        pltpu.VMEM((G, N, d_ssm // G), F32),
                        pltpu.VMEM((3 * LANES, 2 * d_ssm), BF16)],
        compiler_params=_cparams(("parallel", "arbitrary")),
        name="mixer",
    )(proj, proj, proj, proj, proj, proj, dt_raw, conv_w, conv_b, sc_conv_w, dt_bias, a_log, d_exp, norm_g)


def _out_proj_kernel(ya_ref, yb_ref, w_ref, x_ref, h_ref, *, tn):
    ya = ya_ref[...]
    yb = yb_ref[...]
    Ka = ya.shape[1]
    for s in range(h_ref.shape[1] // tn):
        sl = slice(s * tn, (s + 1) * tn)
        acc = jnp.dot(ya, w_ref[:Ka, sl], preferred_element_type=F32)
        acc = acc + jnp.dot(yb, w_ref[Ka:, sl], preferred_element_type=F32)
        h_ref[:, sl] = x_ref[:, sl] + acc


def _out_proj(y_ssm, y_sc, w_out, x, tm, tn):
    T, D = x.shape
    Ka = y_ssm.shape[1]
    Kb = y_sc.shape[1]
    return pl.pallas_call(
        functools.partial(_out_proj_kernel, tn=tn),
        grid=(T // tm,),
        in_specs=[
            pl.BlockSpec((tm, Ka), lambda i: (i, 0)),
            pl.BlockSpec((tm, Kb), lambda i: (i, 0)),
            pl.BlockSpec((Ka + Kb, D), lambda i: (0, 0), pipeline_mode=pl.Buffered(1)),
            pl.BlockSpec((tm, D), lambda i: (i, 0)),
        ],
        out_specs=pl.BlockSpec((tm, D), lambda i: (i, 0)),
        out_shape=jax.ShapeDtypeStruct((T, D), F32),
        compiler_params=_cparams(("parallel",)),
        name="out_proj",
    )(y_ssm, y_sc, w_out, x)


def _ffn_kernel(h_ref, g_ref, wg_ref, wu_ref, wd_ref, gf_ref, o_ref, n_scr, *, final_norm, nsplit):
    f = pl.program_id(1)

    @pl.when(f == 0)
    def _():
        h = h_ref[...]
        n_scr[...] = (h * _rms_scale(h) * g_ref[...]).astype(BF16)
        o_ref[...] = h

    n = n_scr[...]
    gate = jnp.dot(n, wg_ref[...], preferred_element_type=F32)
    up = jnp.dot(n, wu_ref[...], preferred_element_type=F32)
    a = (_silu(gate) * up).astype(BF16)
    wn = o_ref.shape[1] // nsplit
    for s in range(nsplit):
        sl = slice(s * wn, (s + 1) * wn)
        o_ref[:, sl] += jnp.dot(a, wd_ref[:, sl], preferred_element_type=F32)

    if final_norm:
        @pl.when(f == pl.num_programs(1) - 1)
        def _():
            h2 = o_ref[...]
            o_ref[...] = h2 * _rms_scale(h2) * gf_ref[...]


def _ffn(h1, g, w_gate, w_up, w_down, g_final, final_norm, tm, tf):
    T, D = h1.shape
    F = w_gate.shape[1]
    return pl.pallas_call(
        functools.partial(_ffn_kernel, final_norm=final_norm, nsplit=4),
        grid=(T // tm, F // tf),
        in_specs=[
            pl.BlockSpec((tm, D), lambda i, f: (i, 0)),
            pl.BlockSpec((1, D), lambda i, f: (0, 0)),
            pl.BlockSpec((D, tf), lambda i, f: (0, f)),
            pl.BlockSpec((D, tf), lambda i, f: (0, f)),
            pl.BlockSpec((tf, D), lambda i, f: (f, 0)),
            pl.BlockSpec((1, D), lambda i, f: (0, 0)),
        ],
        out_specs=pl.BlockSpec((tm, D), lambda i, f: (i, 0)),
        out_shape=jax.ShapeDtypeStruct((T, D), F32),
        scratch_shapes=[pltpu.VMEM((tm, D), BF16)],
        compiler_params=_cparams(("parallel", "arbitrary")),
        name="ffn",
    )(h1, g, w_gate, w_up, w_down, g_final)


def _pad_lanes(v):
    return jnp.pad(v.reshape(1, -1), ((0, 0), (0, LANES - v.shape[-1])))


def kernel(x, norm_mix_g, w_in, ssm_conv_w, ssm_conv_b, ssm_dt_bias, ssm_A_log, ssm_D, ssm_norm_g,
           sc_conv_w, w_out, norm_ffn_g, w_gate, w_up, w_down, norm_final_g):
    batch, seq, d_model = x.shape
    depth = w_in.shape[0]
    d_ssm = ssm_norm_g.shape[1]
    d_xbc = ssm_conv_w.shape[2]
    heads = ssm_dt_bias.shape[1]
    d_conv = sc_conv_w.shape[2]
    assert d_ssm == d_conv == d_model and heads * SSM_HEADDIM == d_ssm and 3 * heads <= LANES
    assert seq % 1024 == 0
    off_dt = d_ssm + d_xbc
    off_cb = off_dt + heads
    assert w_in.shape[2] - off_cb == off_dt

    h = x.reshape(batch * seq, d_model)
    for l in range(depth):
        w_proj, w_dt = _prep_w_in(w_in[l].T, off_dt, heads, tn=512)
        proj, dt_raw, (wo_b, wg_b, wu_b, wd_b) = _in_proj(
            h, norm_mix_g[l].reshape(1, -1), w_proj, w_dt, (w_out[l], w_gate[l], w_up[l], w_down[l]),
            tm=1024, tn=1024)
        y_ssm, y_sc = _mixer(proj, dt_raw, ssm_conv_w[l], ssm_conv_b[l].reshape(1, -1), sc_conv_w[l],
                             _pad_lanes(ssm_dt_bias[l]), _pad_lanes(ssm_A_log[l]),
                             jnp.repeat(ssm_D[l], SSM_HEADDIM).reshape(1, -1), ssm_norm_g[l].reshape(1, -1),
                             batch, seq, heads, tm=512)
        h1 = _out_proj(y_ssm, y_sc, wo_b, h, tm=512, tn=512)
        last = l == depth - 1
        h = _ffn(h1, norm_ffn_g[l].reshape(1, -1), wg_b, wu_b, wd_b, norm_final_g.reshape(1, -1),
                 final_norm=last, tm=1024, tf=512)
    return h.reshape(batch, seq, d_model)
```

```python
import functools

import jax
import jax.numpy as jnp
from jax import lax
from jax.experimental import pallas as pl
from jax.experimental.pallas import tpu as pltpu

F32 = jnp.float32
BF16 = jnp.bfloat16

EPS = 1e-5
LOG2E = 1.4426950408889634
SSM_HEADDIM = 64
SSM_GROUPS = 8
SSM_STATE = 128
SSM_CONV = 4
SHORT_CONV = 3
CHUNK = 128
SUBLANES = 8
LANES = 128
BF16_ROWS = 16
VMEM_LIMIT = 60 * 1024 * 1024


def _cparams(sem):
    return pltpu.CompilerParams(dimension_semantics=sem, vmem_limit_bytes=VMEM_LIMIT)


def _rms_scale(x):
    return lax.rsqrt(jnp.mean(x * x, axis=-1, keepdims=True) + EPS)


def _silu_from_half(h):
    return h + h * jnp.tanh(h)


def _silu(x):
    return _silu_from_half(0.5 * x)


def _split3(v):
    hi = v.astype(BF16)
    r1 = v - hi.astype(F32)
    mid = r1.astype(BF16)
    lo = (r1 - mid.astype(F32)).astype(BF16)
    return hi, mid, lo


def _prep_kernel(blk_ref, nxt_ref, w_ref, wdt_ref, *, nt, shift):
    k = pl.program_id(0)
    tn = blk_ref.shape[0]

    @pl.when(k < nt)
    def _():
        w_ref[...] = blk_ref[...].astype(BF16)

    @pl.when(k >= nt)
    def _():
        w_ref[:tn - shift, :] = blk_ref[shift:, :].astype(BF16)
        w_ref[tn - shift:, :] = nxt_ref[...].astype(BF16)

    @pl.when(k == nt)
    def _():
        wdt_ref[:shift, :] = blk_ref[:shift, :].astype(BF16)
        wdt_ref[shift:, :] = jnp.zeros((LANES - shift, wdt_ref.shape[1]), BF16)


def _prep_w_in(wt, off_b, shift, tn):
    D = wt.shape[1]
    nt = off_b // tn
    assert off_b % tn == 0 and tn % shift == 0 and shift % BF16_ROWS == 0 and wt.shape[0] == 2 * off_b + shift
    return pl.pallas_call(
        functools.partial(_prep_kernel, nt=nt, shift=shift),
        grid=(2 * nt,),
        in_specs=[
            pl.BlockSpec((tn, D), lambda k: (k, 0)),
            pl.BlockSpec((shift, D), lambda k: ((k + 1) * (tn // shift), 0)),
        ],
        out_specs=[
            pl.BlockSpec((tn, D), lambda k: (k, 0)),
            pl.BlockSpec((LANES, D), lambda k: (0, 0)),
        ],
        out_shape=[jax.ShapeDtypeStruct((2 * off_b, D), BF16), jax.ShapeDtypeStruct((LANES, D), BF16)],
        compiler_params=_cparams(("arbitrary",)),
        name="prep_w_in",
    )(wt, wt)


def _in_proj_kernel(*refs, nside):
    x_ref, g_ref, w_ref, wdt_ref = refs[:4]
    side_in = refs[4:4 + nside]
    out_ref, dt_ref = refs[4 + nside:6 + nside]
    side_out = refs[6 + nside:6 + 2 * nside]
    n_scr = refs[6 + 2 * nside]

    def mm(lhs, wt_ref):
        return lax.dot_general(lhs, wt_ref[...], (((1,), (1,)), ((), ())), preferred_element_type=F32)

    @pl.when(pl.program_id(1) == 0)
    def _():
        x = x_ref[...]
        n = (x * _rms_scale(x) * g_ref[...]).astype(BF16)
        n_scr[...] = n
        dt_ref[...] = mm(n, wdt_ref)

    out_ref[...] = mm(n_scr[...], w_ref).astype(BF16)
    for src, dst in zip(side_in, side_out):
        dst[...] = src[...].astype(BF16)


def _slab_rows(rows, steps):
    per = -(-rows // steps)
    per = -(-per // BF16_ROWS) * BF16_ROWS
    assert rows % per == 0, (rows, per)
    return per


def _in_proj(x, g, w, w_dt, side, tm, tn):
    T, D = x.shape
    N = w.shape[0]
    nj = N // tn
    steps = (T // tm) * nj
    side_specs = []
    for s in side:
        rb = _slab_rows(s.shape[0], steps)
        last = s.shape[0] // rb - 1
        side_specs.append(pl.BlockSpec((rb, s.shape[1]),
                                       lambda i, j, last=last: (jnp.minimum(i * nj + j, last), 0)))
    outs = pl.pallas_call(
        functools.partial(_in_proj_kernel, nside=len(side)),
        grid=(T // tm, nj),
        in_specs=[
            pl.BlockSpec((tm, D), lambda i, j: (i, 0)),
            pl.BlockSpec((1, D), lambda i, j: (0, 0)),
            pl.BlockSpec((tn, D), lambda i, j: (j, 0)),
            pl.BlockSpec((LANES, D), lambda i, j: (0, 0)),
        ] + side_specs,
        out_specs=[
            pl.BlockSpec((tm, tn), lambda i, j: (i, j)),
            pl.BlockSpec((tm, LANES), lambda i, j: (i, 0)),
        ] + side_specs,
        out_shape=[
            jax.ShapeDtypeStruct((T, N), BF16),
            jax.ShapeDtypeStruct((T, LANES), F32),
        ] + [jax.ShapeDtypeStruct(s.shape, BF16) for s in side],
        scratch_shapes=[pltpu.VMEM((tm, D), BF16)],
        compiler_params=_cparams(("arbitrary", "arbitrary")),
        name="in_proj",
    )(x, g, w, w_dt, *side)
    return outs[0], outs[1], outs[2:]


def _conv_even_odd(buf, c, w, width, rows):
    half = rows // 2
    shifted = {s: buf[c, pl.ds(SUBLANES - s, half, stride=2), :] for s in range(-1, width)}
    even = odd = None
    for k in range(width):
        tap = w[k:k + 1, :]
        e = tap * shifted[width - 1 - k]
        o = tap * shifted[width - 2 - k]
        even = e if even is None else even + e
        odd = o if odd is None else odd + o
    return even, odd


def _store_even_odd(dst, c, even, odd, rows):
    half = rows // 2
    dst[c, pl.ds(0, half, stride=2), :] = even
    dst[c, pl.ds(1, half, stride=2), :] = odd


def _mixer_kernel(z_ref, xs_ref, bc_ref, gb_ref, gc_ref, u_ref, dt_ref, cw_ref, cb_ref, scw_ref, dtb_ref,
                  alog_ref, dexp_ref, ng_ref, y_ref, ysc_ref, buf, act, pbuf, pact, state, sel, *, heads):
    L = CHUNK
    G, N, P = SSM_GROUPS, SSM_STATE, SSM_HEADDIM
    R = heads // G
    GW = R * P
    nx = heads * P // LANES
    nsc = pbuf.shape[0]
    lane = lambda c: slice(c * LANES, (c + 1) * LANES)

    @pl.when(pl.program_id(1) == 0)
    def _():
        buf[:, 0:SUBLANES, :] = jnp.zeros((buf.shape[0], SUBLANES, LANES), F32)
        pbuf[:, 0:SUBLANES, :] = jnp.zeros((nsc, SUBLANES, LANES), F32)
        state[...] = jnp.zeros(state.shape, F32)
        src_lane = lax.broadcasted_iota(jnp.int32, (LANES, 2 * heads * P), 0)
        out_col = lax.broadcasted_iota(jnp.int32, (LANES, 2 * heads * P), 1)
        grp = out_col // (2 * GW)
        kind = (out_col // GW) % 2
        head = grp * R + (out_col % GW) // P
        one = (src_lane == heads * (1 + kind) + head).astype(BF16)
        for k in range(3):
            sel[k * LANES:(k + 1) * LANES, :] = one

    def chunk(ci, carry):
        rows = pl.ds(pl.multiple_of(ci * L, L), L)

        dt_in = (dt_ref[rows, :] + dtb_ref[...]).T[:heads]
        dt = jnp.maximum(dt_in, 0.0) + jnp.log1p(jnp.exp(-jnp.abs(dt_in)))
        a2 = jnp.broadcast_to(-LOG2E * jnp.exp(alog_ref[...]), (L, LANES)).T[:heads]
        dA2 = dt * a2
        row = lax.broadcasted_iota(jnp.int32, (L, L), 0)
        col = lax.broadcasted_iota(jnp.int32, (L, L), 1)
        causal = row >= col
        triu = (row <= col).astype(BF16)
        cs2 = jnp.dot(jnp.concatenate(_split3(dA2), axis=1), jnp.concatenate([triu, triu, triu], axis=0),
                      preferred_element_type=F32)
        cs2_last = cs2[:, L - 1:L]
        ecs = jnp.exp2(cs2)
        wdec = dt * jnp.exp2(cs2_last - cs2)
        csd = cs2 - jnp.log2(dt)
        colT = jnp.concatenate([cs2, wdec, ecs, jnp.zeros((LANES - 3 * heads, L), F32)], axis=0).T
        colT3 = jnp.concatenate(_split3(colT), axis=1)

        for c in range(buf.shape[0]):
            src = xs_ref[rows, lane(c)] if c < nx else bc_ref[rows, lane(c - nx)]
            buf[c, SUBLANES:, :] = src.astype(F32)
        for c in range(buf.shape[0]):
            even, odd = _conv_even_odd(buf, c, 0.5 * cw_ref[:, lane(c)], SSM_CONV, L)
            b = 0.5 * cb_ref[:, lane(c)]
            _store_even_odd(act, c, _silu_from_half(even + b), _silu_from_half(odd + b), L)
            buf[c, 0:SUBLANES, :] = buf[c, L:L + SUBLANES, :]

        head_of_lane = lax.broadcasted_iota(jnp.int32, (L, GW), 1) // P
        ygs = []
        ssq = jnp.zeros((L, 1), F32)
        for g in range(G):
            sl = slice(g * GW, (g + 1) * GW)
            xs = jnp.concatenate([act[g * GW // LANES + k] for k in range(GW // LANES)], axis=1)
            xs_b = xs.astype(BF16)
            Bg = act[nx + g].astype(BF16)
            Cg = act[nx + G + g].astype(BF16)
            CB = lax.dot_general(Cg, Bg, (((1,), (1,)), ((), ())), preferred_element_type=F32)
            Hs = state[g]
            y_off = jnp.dot(Cg, Hs.astype(BF16), preferred_element_type=F32)
            Ms, xbd = [], []
            for r in range(R):
                h = g * R + r
                seg = colT[:, h:h + 1] - csd[h:h + 1, :]
                Ms.append((CB * jnp.exp2(jnp.where(causal, seg, -jnp.inf))).astype(BF16))
                xbd.append(jnp.where(head_of_lane == r, xs_b, jnp.zeros_like(xs_b)))
            y_diag = jnp.dot(jnp.concatenate(Ms, axis=1), jnp.concatenate(xbd, axis=0),
                             preferred_element_type=F32)
            expd = jnp.dot(colT3, sel[:, 2 * g * GW:2 * (g + 1) * GW], preferred_element_type=F32)
            wdec_e = expd[:, :GW]
            ecs_e = expd[:, GW:]
            xt_b = (xs * wdec_e).astype(BF16)
            st_new = lax.dot_general(Bg, xt_b, (((0,), (0,)), ((), ())), preferred_element_type=F32)
            state[g] = Hs * ecs_e[L - 1:L, :] + st_new
            y = y_diag + y_off * ecs_e + dexp_ref[:, sl] * xs
            yg = y * _silu(z_ref[rows, sl].astype(F32))
            ssq = ssq + jnp.sum(yg * yg, axis=-1, keepdims=True)
            ygs.append(yg)
        scale = lax.rsqrt(ssq * (1.0 / (G * GW)) + EPS)
        for g in range(G):
            sl = slice(g * GW, (g + 1) * GW)
            y_ref[rows, sl] = (ygs[g] * scale * ng_ref[:, sl]).astype(BF16)

        for c in range(nsc):
            pbuf[c, SUBLANES:, :] = gc_ref[rows, lane(c)].astype(F32) * u_ref[rows, lane(c)].astype(F32)
        for c in range(nsc):
            even, odd = _conv_even_odd(pbuf, c, scw_ref[:, lane(c)], SHORT_CONV, L)
            _store_even_odd(pact, c, even, odd, L)
            pbuf[c, 0:SUBLANES, :] = pbuf[c, L:L + SUBLANES, :]
        for c in range(nsc):
            ysc_ref[rows, lane(c)] = (gb_ref[rows, lane(c)].astype(F32) * pact[c]).astype(BF16)
        return carry

    lax.fori_loop(0, z_ref.shape[0] // L, chunk, 0)


def _mixer(proj, dt_raw, conv_w, conv_b, sc_conv_w, dt_bias, a_log, d_exp, norm_g, batch, seq, heads, tm):
    T = proj.shape[0]
    nt = seq // tm
    G, N = SSM_GROUPS, SSM_STATE
    d_ssm = heads * SSM_HEADDIM
    d_xbc = conv_w.shape[1]
    d_conv = sc_conv_w.shape[1]
    assert d_xbc - d_ssm == 2 * G * N == d_ssm == d_conv and tm % CHUNK == 0 and seq % tm == 0
    blk = lambda k: pl.BlockSpec((tm, d_ssm), lambda b, t: (b * nt + t, k))
    const = lambda b, t: (0, 0)
    return pl.pallas_call(
        functools.partial(_mixer_kernel, heads=heads),
        grid=(batch, nt),
        in_specs=[
            blk(0), blk(1), blk(2), blk(3), blk(4), blk(5),
            pl.BlockSpec((tm, LANES), lambda b, t: (b * nt + t, 0)),
            pl.BlockSpec((SSM_CONV, d_xbc), const),
            pl.BlockSpec((1, d_xbc), const),
            pl.BlockSpec((SHORT_CONV, d_conv), const),
            pl.BlockSpec((1, LANES), const),
            pl.BlockSpec((1, LANES), const),
            pl.BlockSpec((1, d_ssm), const),
            pl.BlockSpec((1, d_ssm), const),
        ],
        out_specs=[blk(0), blk(0)],
        out_shape=[jax.ShapeDtypeStruct((T, d_ssm), BF16), jax.ShapeDtypeStruct((T, d_conv), BF16)],
        scratch_shapes=[pltpu.VMEM((d_xbc // LANES, SUBLANES + CHUNK, LANES), F32),
                        pltpu.VMEM((d_xbc // LANES, CHUNK, LANES), F32),
                        pltpu.VMEM((d_conv // LANES, SUBLANES + CHUNK, LANES), F32),
                        pltpu.VMEM((d_conv // LANES, CHUNK, LANES), F32),
                        pltpu.VMEM((G, N, d_ssm // G), F32),
                        pltpu.VMEM((3 * LANES, 2 * d_ssm), BF16)],
        compiler_params=_cparams(("parallel", "arbitrary")),
        name="mixer",
    )(proj, proj, proj, proj, proj, proj, dt_raw, conv_w, conv_b, sc_conv_w, dt_bias, a_log, d_exp, norm_g)


def _out_proj_kernel(ya_ref, yb_ref, w_ref, x_ref, h_ref, *, tn):
    ya = ya_ref[...]
    yb = yb_ref[...]
    Ka = ya.shape[1]
    for s in range(h_ref.shape[1] // tn):
        sl = slice(s * tn, (s + 1) * tn)
        acc = jnp.dot(ya, w_ref[:Ka, sl], preferred_element_type=F32)
        acc = acc + jnp.dot(yb, w_ref[Ka:, sl], preferred_element_type=F32)
        h_ref[:, sl] = x_ref[:, sl] + acc


def _out_proj(y_ssm, y_sc, w_out, x, tm, tn):
    T, D = x.shape
    Ka = y_ssm.shape[1]
    Kb = y_sc.shape[1]
    return pl.pallas_call(
        functools.partial(_out_proj_kernel, tn=tn),
        grid=(T // tm,),
        in_specs=[
            pl.BlockSpec((tm, Ka), lambda i: (i, 0)),
            pl.BlockSpec((tm, Kb), lambda i: (i, 0)),
            pl.BlockSpec((Ka + Kb, D), lambda i: (0, 0), pipeline_mode=pl.Buffered(1)),
            pl.BlockSpec((tm, D), lambda i: (i, 0)),
        ],
        out_specs=pl.BlockSpec((tm, D), lambda i: (i, 0)),
        out_shape=jax.ShapeDtypeStruct((T, D), F32),
        compiler_params=_cparams(("parallel",)),
        name="out_proj",
    )(y_ssm, y_sc, w_out, x)


def _ffn_kernel(h_ref, g_ref, wg_ref, wu_ref, wd_ref, gf_ref, o_ref, n_scr, *, final_norm, nsplit):
    f = pl.program_id(1)

    @pl.when(f == 0)
    def _():
        h = h_ref[...]
        n_scr[...] = (h * _rms_scale(h) * g_ref[...]).astype(BF16)
        o_ref[...] = h

    n = n_scr[...]
    gate = jnp.dot(n, wg_ref[...], preferred_element_type=F32)
    up = jnp.dot(n, wu_ref[...], preferred_element_type=F32)
    a = (_silu(gate) * up).astype(BF16)
    wn = o_ref.shape[1] // nsplit
    for s in range(nsplit):
        sl = slice(s * wn, (s + 1) * wn)
        o_ref[:, sl] += jnp.dot(a, wd_ref[:, sl], preferred_element_type=F32)

    if final_norm:
        @pl.when(f == pl.num_programs(1) - 1)
        def _():
            h2 = o_ref[...]
            o_ref[...] = h2 * _rms_scale(h2) * gf_ref[...]


def _ffn(h1, g, w_gate, w_up, w_down, g_final, final_norm, tm, tf):
    T, D = h1.shape
    F = w_gate.shape[1]
    return pl.pallas_call(
        functools.partial(_ffn_kernel, final_norm=final_norm, nsplit=4),
        grid=(T // tm, F // tf),
        in_specs=[
            pl.BlockSpec((tm, D), lambda i, f: (i, 0)),
            pl.BlockSpec((1, D), lambda i, f: (0, 0)),
            pl.BlockSpec((D, tf), lambda i, f: (0, f)),
            pl.BlockSpec((D, tf), lambda i, f: (0, f)),
            pl.BlockSpec((tf, D), lambda i, f: (f, 0)),
            pl.BlockSpec((1, D), lambda i, f: (0, 0)),
        ],
        out_specs=pl.BlockSpec((tm, D), lambda i, f: (i, 0)),
        out_shape=jax.ShapeDtypeStruct((T, D), F32),
        scratch_shapes=[pltpu.VMEM((tm, D), BF16)],
        compiler_params=_cparams(("parallel", "arbitrary")),
        name="ffn",
    )(h1, g, w_gate, w_up, w_down, g_final)


def _pad_lanes(v):
    return jnp.pad(v.reshape(1, -1), ((0, 0), (0, LANES - v.shape[-1])))


def kernel(x, norm_mix_g, w_in, ssm_conv_w, ssm_conv_b, ssm_dt_bias, ssm_A_log, ssm_D, ssm_norm_g,
           sc_conv_w, w_out, norm_ffn_g, w_gate, w_up, w_down, norm_final_g):
    batch, seq, d_model = x.shape
    depth = w_in.shape[0]
    d_ssm = ssm_norm_g.shape[1]
    d_xbc = ssm_conv_w.shape[2]
    heads = ssm_dt_bias.shape[1]
    d_conv = sc_conv_w.shape[2]
    assert d_ssm == d_conv == d_model and heads * SSM_HEADDIM == d_ssm and 3 * heads <= LANES
    assert seq % 1024 == 0
    off_dt = d_ssm + d_xbc
    off_cb = off_dt + heads
    assert w_in.shape[2] - off_cb == off_dt

    h = x.reshape(batch * seq, d_model)
    for l in range(depth):
        w_proj, w_dt = _prep_w_in(w_in[l].T, off_dt, heads, tn=512)
        proj, dt_raw, (wo_b, wg_b, wu_b, wd_b) = _in_proj(
            h, norm_mix_g[l].reshape(1, -1), w_proj, w_dt, (w_out[l], w_gate[l], w_up[l], w_down[l]),
            tm=1024, tn=1024)
        y_ssm, y_sc = _mixer(proj, dt_raw, ssm_conv_w[l], ssm_conv_b[l].reshape(1, -1), sc_conv_w[l],
                             _pad_lanes(ssm_dt_bias[l]), _pad_lanes(ssm_A_log[l]),
                             jnp.repeat(ssm_D[l], SSM_HEADDIM).reshape(1, -1), ssm_norm_g[l].reshape(1, -1),
                             batch, seq, heads, tm=512)
        h1 = _out_proj(y_ssm, y_sc, wo_b, h, tm=512, tn=512)
        last = l == depth - 1
        h = _ffn(h1, norm_ffn_g[l].reshape(1, -1), wg_b, wu_b, wd_b, norm_final_g.reshape(1, -1),
                 final_norm=last, tm=1024, tf=512)
    return h.reshape(batch, seq, d_model)
```

```python
import functools

import jax
import jax.numpy as jnp
from jax import lax
from jax.experimental import pallas as pl
from jax.experimental.pallas import tpu as pltpu

F32 = jnp.float32
BF16 = jnp.bfloat16

EPS = 1e-5
LOG2E = 1.4426950408889634
SSM_HEADDIM = 64
SSM_GROUPS = 8
SSM_STATE = 128
SSM_CONV = 4
SHORT_CONV = 3
CHUNK = 128
SUBLANES = 8
LANES = 128
BF16_ROWS = 16
VMEM_LIMIT = 60 * 1024 * 1024


def _cparams(sem):
    return pltpu.CompilerParams(dimension_semantics=sem, vmem_limit_bytes=VMEM_LIMIT)


def _rms_scale(x):
    return lax.rsqrt(jnp.mean(x * x, axis=-1, keepdims=True) + EPS)


def _silu_from_half(h):
    return h + h * jnp.tanh(h)


def _silu(x):
    return _silu_from_half(0.5 * x)


def _split3(v):
    hi = v.astype(BF16)
    r1 = v - hi.astype(F32)
    mid = r1.astype(BF16)
    lo = (r1 - mid.astype(F32)).astype(BF16)
    return hi, mid, lo


def _prep_kernel(blk_ref, nxt_ref, w_ref, wdt_ref, *, nt, shift):
    k = pl.program_id(0)
    tn = blk_ref.shape[0]

    @pl.when(k < nt)
    def _():
        w_ref[...] = blk_ref[...].astype(BF16)

    @pl.when(k >= nt)
    def _():
        w_ref[:tn - shift, :] = blk_ref[shift:, :].astype(BF16)
        w_ref[tn - shift:, :] = nxt_ref[...].astype(BF16)

    @pl.when(k == nt)
    def _():
        wdt_ref[:shift, :] = blk_ref[:shift, :].astype(BF16)
        wdt_ref[shift:, :] = jnp.zeros((LANES - shift, wdt_ref.shape[1]), BF16)


def _prep_w_in(wt, off_b, shift, tn):
    D = wt.shape[1]
    nt = off_b // tn
    assert off_b % tn == 0 and tn % shift == 0 and shift % BF16_ROWS == 0 and wt.shape[0] == 2 * off_b + shift
    return pl.pallas_call(
        functools.partial(_prep_kernel, nt=nt, shift=shift),
        grid=(2 * nt,),
        in_specs=[
            pl.BlockSpec((tn, D), lambda k: (k, 0)),
            pl.BlockSpec((shift, D), lambda k: ((k + 1) * (tn // shift), 0)),
        ],
        out_specs=[
            pl.BlockSpec((tn, D), lambda k: (k, 0)),
            pl.BlockSpec((LANES, D), lambda k: (0, 0)),
        ],
        out_shape=[jax.ShapeDtypeStruct((2 * off_b, D), BF16), jax.ShapeDtypeStruct((LANES, D), BF16)],
        compiler_params=_cparams(("arbitrary",)),
        name="prep_w_in",
    )(wt, wt)


def _in_proj_kernel(x_ref, g_ref, w_ref, wdt_ref, out_ref, dt_ref, n_scr, *, ncol):
    def mm(lhs, wt):
        return lax.dot_general(lhs, wt, (((1,), (1,)), ((), ())), preferred_element_type=F32)

    @pl.when(pl.program_id(1) == 0)
    def _():
        x = x_ref[...]
        n = (x * _rms_scale(x) * g_ref[...]).astype(BF16)
        n_scr[...] = n
        dt_ref[...] = mm(n, wdt_ref[...])

    for s in range(out_ref.shape[1] // ncol):
        cols = slice(s * ncol, (s + 1) * ncol)
        out_ref[:, cols] = mm(n_scr[...], w_ref[cols, :]).astype(BF16)


def _in_proj(x, g, w, w_dt, tm, tn):
    T, D = x.shape
    N = w.shape[0]
    return pl.pallas_call(
        functools.partial(_in_proj_kernel, ncol=min(tn, 1024)),
        grid=(T // tm, N // tn),
        in_specs=[
            pl.BlockSpec((tm, D), lambda i, j: (i, 0)),
            pl.BlockSpec((1, D), lambda i, j: (0, 0)),
            pl.BlockSpec((tn, D), lambda i, j: (j, 0)),
            pl.BlockSpec((LANES, D), lambda i, j: (0, 0)),
        ],
        out_specs=[
            pl.BlockSpec((tm, tn), lambda i, j: (i, j)),
            pl.BlockSpec((tm, LANES), lambda i, j: (i, 0)),
        ],
        out_shape=[jax.ShapeDtypeStruct((T, N), BF16), jax.ShapeDtypeStruct((T, LANES), F32)],
        scratch_shapes=[pltpu.VMEM((tm, D), BF16)],
        compiler_params=_cparams(("parallel", "arbitrary")),
        name="in_proj",
    )(x, g, w, w_dt)


def _slab_rows(rows, steps):
    per = -(-rows // steps)
    per = -(-per // BF16_ROWS) * BF16_ROWS
    while rows % per:
        per += BF16_ROWS
    return per


def _conv_even_odd(buf, c, w, width, rows):
    half = rows // 2
    shifted = {s: buf[c, pl.ds(SUBLANES - s, half, stride=2), :] for s in range(-1, width)}
    even = odd = None
    for k in range(width):
        tap = w[k:k + 1, :]
        e = tap * shifted[width - 1 - k]
        o = tap * shifted[width - 2 - k]
        even = e if even is None else even + e
        odd = o if odd is None else odd + o
    return even, odd


def _store_even_odd(dst, c, even, odd, rows):
    half = rows // 2
    dst[c, pl.ds(0, half, stride=2), :] = even
    dst[c, pl.ds(1, half, stride=2), :] = odd


def _mixer_kernel(*refs, heads, nside):
    (z_ref, xs_ref, bc_ref, gb_ref, gc_ref, u_ref, dt_ref, cw_ref, cb_ref, scw_ref, dtb_ref, alog_ref, dexp_ref,
     ng_ref) = refs[:14]
    side_in = refs[14:14 + nside]
    y_ref, ysc_ref = refs[14 + nside:16 + nside]
    side_out = refs[16 + nside:16 + 2 * nside]
    buf, act, pbuf, pact, state, sel = refs[16 + 2 * nside:]
    L = CHUNK
    for src, dst in zip(side_in, side_out):
        dst[...] = src[...].astype(BF16)
    G, N, P = SSM_GROUPS, SSM_STATE, SSM_HEADDIM
    R = heads // G
    GW = R * P
    nx = heads * P // LANES
    nsc = pbuf.shape[0]
    lane = lambda c: slice(c * LANES, (c + 1) * LANES)

    @pl.when(pl.program_id(1) == 0)
    def _():
        buf[:, 0:SUBLANES, :] = jnp.zeros((buf.shape[0], SUBLANES, LANES), F32)
        pbuf[:, 0:SUBLANES, :] = jnp.zeros((nsc, SUBLANES, LANES), F32)
        state[...] = jnp.zeros(state.shape, F32)
        src_lane = lax.broadcasted_iota(jnp.int32, (LANES, 2 * heads * P), 0)
        out_col = lax.broadcasted_iota(jnp.int32, (LANES, 2 * heads * P), 1)
        grp = out_col // (2 * GW)
        kind = (out_col // GW) % 2
        head = grp * R + (out_col % GW) // P
        one = (src_lane == heads * (1 + kind) + head).astype(BF16)
        for k in range(3):
            sel[k * LANES:(k + 1) * LANES, :] = one

    def chunk(ci, carry):
        rows = pl.ds(pl.multiple_of(ci * L, L), L)

        dt_in = (dt_ref[rows, :] + dtb_ref[...]).T[:heads]
        dt = jnp.maximum(dt_in, 0.0) + jnp.log1p(jnp.exp(-jnp.abs(dt_in)))
        a2 = jnp.broadcast_to(-LOG2E * jnp.exp(alog_ref[...]), (L, LANES)).T[:heads]
        dA2 = dt * a2
        row = lax.broadcasted_iota(jnp.int32, (L, L), 0)
        col = lax.broadcasted_iota(jnp.int32, (L, L), 1)
        causal = row >= col
        triu = (row <= col).astype(BF16)
        cs2 = jnp.dot(jnp.concatenate(_split3(dA2), axis=1), jnp.concatenate([triu, triu, triu], axis=0),
                      preferred_element_type=F32)
        cs2_last = cs2[:, L - 1:L]
        ecs = jnp.exp2(cs2)
        wdec = dt * jnp.exp2(cs2_last - cs2)
        csd = cs2 - jnp.log2(dt)
        colT = jnp.concatenate([cs2, wdec, ecs, jnp.zeros((LANES - 3 * heads, L), F32)], axis=0).T
        colT3 = jnp.concatenate(_split3(colT), axis=1)

        for c in range(buf.shape[0]):
            src = xs_ref[rows, lane(c)] if c < nx else bc_ref[rows, lane(c - nx)]
            buf[c, SUBLANES:, :] = src.astype(F32)
        for c in range(buf.shape[0]):
            even, odd = _conv_even_odd(buf, c, 0.5 * cw_ref[:, lane(c)], SSM_CONV, L)
            b = 0.5 * cb_ref[:, lane(c)]
            _store_even_odd(act, c, _silu_from_half(even + b), _silu_from_half(odd + b), L)
            buf[c, 0:SUBLANES, :] = buf[c, L:L + SUBLANES, :]

        head_of_lane = lax.broadcasted_iota(jnp.int32, (L, GW), 1) // P
        ygs = []
        ssq = jnp.zeros((L, 1), F32)
        for g in range(G):
            sl = slice(g * GW, (g + 1) * GW)
            xs = jnp.concatenate([act[g * GW // LANES + k] for k in range(GW // LANES)], axis=1)
            xs_b = xs.astype(BF16)
            Bg = act[nx + g].astype(BF16)
            Cg = act[nx + G + g].astype(BF16)
            CB = lax.dot_general(Cg, Bg, (((1,), (1,)), ((), ())), preferred_element_type=F32)
            Hs = state[g]
            y_off = jnp.dot(Cg, Hs.astype(BF16), preferred_element_type=F32)
            Ms, xbd = [], []
            for r in range(R):
                h = g * R + r
                seg = colT[:, h:h + 1] - csd[h:h + 1, :]
                Ms.append((CB * jnp.exp2(jnp.where(causal, seg, -jnp.inf))).astype(BF16))
                xbd.append(jnp.where(head_of_lane == r, xs_b, jnp.zeros_like(xs_b)))
            y_diag = jnp.dot(jnp.concatenate(Ms, axis=1), jnp.concatenate(xbd, axis=0),
                             preferred_element_type=F32)
            expd = jnp.dot(colT3, sel[:, 2 * g * GW:2 * (g + 1) * GW], preferred_element_type=F32)
            wdec_e = expd[:, :GW]
            ecs_e = expd[:, GW:]
            xt_b = (xs * wdec_e).astype(BF16)
            st_new = lax.dot_general(Bg, xt_b, (((0,), (0,)), ((), ())), preferred_element_type=F32)
            state[g] = Hs * ecs_e[L - 1:L, :] + st_new
            y = y_diag + y_off * ecs_e + dexp_ref[:, sl] * xs
            yg = y * _silu(z_ref[rows, sl].astype(F32))
            ssq = ssq + jnp.sum(yg * yg, axis=-1, keepdims=True)
            ygs.append(yg)
        scale = lax.rsqrt(ssq * (1.0 / (G * GW)) + EPS)
        for g in range(G):
            sl = slice(g * GW, (g + 1) * GW)
            y_ref[rows, sl] = (ygs[g] * scale * ng_ref[:, sl]).astype(BF16)

        for c in range(nsc):
            pbuf[c, SUBLANES:, :] = gc_ref[rows, lane(c)].astype(F32) * u_ref[rows, lane(c)].astype(F32)
        for c in range(nsc):
            even, odd = _conv_even_odd(pbuf, c, scw_ref[:, lane(c)], SHORT_CONV, L)
            _store_even_odd(pact, c, even, odd, L)
            pbuf[c, 0:SUBLANES, :] = pbuf[c, L:L + SUBLANES, :]
        for c in range(nsc):
            ysc_ref[rows, lane(c)] = (gb_ref[rows, lane(c)].astype(F32) * pact[c]).astype(BF16)
        return carry

    lax.fori_loop(0, z_ref.shape[0] // L, chunk, 0)


def _mixer(proj, dt_raw, conv_w, conv_b, sc_conv_w, dt_bias, a_log, d_exp, norm_g, side, batch, seq, heads, tm):
    T = proj.shape[0]
    nt = seq // tm
    G, N = SSM_GROUPS, SSM_STATE
    d_ssm = heads * SSM_HEADDIM
    d_xbc = conv_w.shape[1]
    d_conv = sc_conv_w.shape[1]
    assert d_xbc - d_ssm == 2 * G * N == d_ssm == d_conv and tm % CHUNK == 0 and seq % tm == 0
    blk = lambda k: pl.BlockSpec((tm, d_ssm), lambda b, t: (b * nt + t, k))
    const = lambda b, t: (0, 0)
    side_specs = []
    for s in side:
        rb = _slab_rows(s.shape[0], batch * nt)
        last = s.shape[0] // rb - 1
        side_specs.append(pl.BlockSpec((rb, s.shape[1]),
                                       lambda b, t, last=last: (jnp.minimum(b * nt + t, last), 0)))
    outs = pl.pallas_call(
        functools.partial(_mixer_kernel, heads=heads, nside=len(side)),
        grid=(batch, nt),
        in_specs=[
            blk(0), blk(1), blk(2), blk(3), blk(4), blk(5),
            pl.BlockSpec((tm, LANES), lambda b, t: (b * nt + t, 0)),
            pl.BlockSpec((SSM_CONV, d_xbc), const),
            pl.BlockSpec((1, d_xbc), const),
            pl.BlockSpec((SHORT_CONV, d_conv), const),
            pl.BlockSpec((1, LANES), const),
            pl.BlockSpec((1, LANES), const),
            pl.BlockSpec((1, d_ssm), const),
            pl.BlockSpec((1, d_ssm), const),
        ] + side_specs,
        out_specs=[blk(0), blk(0)] + side_specs,
        out_shape=[jax.ShapeDtypeStruct((T, d_ssm), BF16), jax.ShapeDtypeStruct((T, d_conv), BF16)]
        + [jax.ShapeDtypeStruct(s.shape, BF16) for s in side],
        scratch_shapes=[pltpu.VMEM((d_xbc // LANES, SUBLANES + CHUNK, LANES), F32),
                        pltpu.VMEM((d_xbc // LANES, CHUNK, LANES), F32),
                        pltpu.VMEM((d_conv // LANES, SUBLANES + CHUNK, LANES), F32),
                        pltpu.VMEM((d_conv // LANES, CHUNK, LANES), F32),
                        pltpu.VMEM((G, N, d_ssm // G), F32),
                        pltpu.VMEM((3 * LANES, 2 * d_ssm), BF16)],
        compiler_params=_cparams(("arbitrary", "arbitrary")),
        name="mixer",
    )(proj, proj, proj, proj, proj, proj, dt_raw, conv_w, conv_b, sc_conv_w, dt_bias, a_log, d_exp, norm_g, *side)
    return outs[0], outs[1], outs[2:]


def _out_proj_kernel(ya_ref, yb_ref, w_ref, x_ref, h_ref, *, tn):
    ya = ya_ref[...]
    yb = yb_ref[...]
    Ka = ya.shape[1]
    for s in range(h_ref.shape[1] // tn):
        sl = slice(s * tn, (s + 1) * tn)
        acc = jnp.dot(ya, w_ref[:Ka, sl], preferred_element_type=F32)
        acc = acc + jnp.dot(yb, w_ref[Ka:, sl], preferred_element_type=F32)
        h_ref[:, sl] = x_ref[:, sl] + acc


def _out_proj(y_ssm, y_sc, w_out, x, tm, tn):
    T, D = x.shape
    Ka = y_ssm.shape[1]
    Kb = y_sc.shape[1]
    return pl.pallas_call(
        functools.partial(_out_proj_kernel, tn=tn),
        grid=(T // tm,),
        in_specs=[
            pl.BlockSpec((tm, Ka), lambda i: (i, 0)),
            pl.BlockSpec((tm, Kb), lambda i: (i, 0)),
            pl.BlockSpec((Ka + Kb, D), lambda i: (0, 0), pipeline_mode=pl.Buffered(1)),
            pl.BlockSpec((tm, D), lambda i: (i, 0)),
        ],
        out_specs=pl.BlockSpec((tm, D), lambda i: (i, 0)),
        out_shape=jax.ShapeDtypeStruct((T, D), F32),
        compiler_params=_cparams(("parallel",)),
        name="out_proj",
    )(y_ssm, y_sc, w_out, x)


def _ffn_kernel(h_ref, g_ref, wg_ref, wu_ref, wd_ref, gf_ref, o_ref, n_scr, *, final_norm, nsplit):
    f = pl.program_id(1)

    @pl.when(f == 0)
    def _():
        h = h_ref[...]
        n_scr[...] = (h * _rms_scale(h) * g_ref[...]).astype(BF16)
        o_ref[...] = h

    n = n_scr[...]
    gate = jnp.dot(n, wg_ref[...], preferred_element_type=F32)
    up = jnp.dot(n, wu_ref[...], preferred_element_type=F32)
    a = (_silu(gate) * up).astype(BF16)
    wn = o_ref.shape[1] // nsplit
    for s in range(nsplit):
        sl = slice(s * wn, (s + 1) * wn)
        o_ref[:, sl] += jnp.dot(a, wd_ref[:, sl], preferred_element_type=F32)

    if final_norm:
        @pl.when(f == pl.num_programs(1) - 1)
        def _():
            h2 = o_ref[...]
            o_ref[...] = h2 * _rms_scale(h2) * gf_ref[...]


def _ffn(h1, g, w_gate, w_up, w_down, g_final, final_norm, tm, tf):
    T, D = h1.shape
    F = w_gate.shape[1]
    return pl.pallas_call(
        functools.partial(_ffn_kernel, final_norm=final_norm, nsplit=4),
        grid=(T // tm, F // tf),
        in_specs=[
            pl.BlockSpec((tm, D), lambda i, f: (i, 0)),
            pl.BlockSpec((1, D), lambda i, f: (0, 0)),
            pl.BlockSpec((D, tf), lambda i, f: (0, f)),
            pl.BlockSpec((D, tf), lambda i, f: (0, f)),
            pl.BlockSpec((tf, D), lambda i, f: (f, 0)),
            pl.BlockSpec((1, D), lambda i, f: (0, 0)),
        ],
        out_specs=pl.BlockSpec((tm, D), lambda i, f: (i, 0)),
        out_shape=jax.ShapeDtypeStruct((T, D), F32),
        scratch_shapes=[pltpu.VMEM((tm, D), BF16)],
        compiler_params=_cparams(("parallel", "arbitrary")),
        name="ffn",
    )(h1, g, w_gate, w_up, w_down, g_final)


def _pad_lanes(v):
    return jnp.pad(v.reshape(1, -1), ((0, 0), (0, LANES - v.shape[-1])))


def kernel(x, norm_mix_g, w_in, ssm_conv_w, ssm_conv_b, ssm_dt_bias, ssm_A_log, ssm_D, ssm_norm_g,
           sc_conv_w, w_out, norm_ffn_g, w_gate, w_up, w_down, norm_final_g):
    batch, seq, d_model = x.shape
    depth = w_in.shape[0]
    d_ssm = ssm_norm_g.shape[1]
    d_xbc = ssm_conv_w.shape[2]
    heads = ssm_dt_bias.shape[1]
    d_conv = sc_conv_w.shape[2]
    assert d_ssm == d_conv == d_model and heads * SSM_HEADDIM == d_ssm and 3 * heads <= LANES
    assert seq % 1024 == 0
    off_dt = d_ssm + d_xbc
    off_cb = off_dt + heads
    assert w_in.shape[2] - off_cb == off_dt

    h = x.reshape(batch * seq, d_model)
    for l in range(depth):
        w_proj, w_dt = _prep_w_in(w_in[l].T, off_dt, heads, tn=512)
        proj, dt_raw = _in_proj(h, norm_mix_g[l].reshape(1, -1), w_proj, w_dt, tm=1024, tn=2048)
        y_ssm, y_sc, (wo_b, wg_b, wu_b, wd_b) = _mixer(
            proj, dt_raw, ssm_conv_w[l], ssm_conv_b[l].reshape(1, -1), sc_conv_w[l], _pad_lanes(ssm_dt_bias[l]),
            _pad_lanes(ssm_A_log[l]), jnp.repeat(ssm_D[l], SSM_HEADDIM).reshape(1, -1),
            ssm_norm_g[l].reshape(1, -1), (w_out[l], w_gate[l], w_up[l], w_down[l]), batch, seq, heads, tm=256)
        h1 = _out_proj(y_ssm, y_sc, wo_b, h, tm=512, tn=512)
        last = l == depth - 1
        h = _ffn(h1, norm_ffn_g[l].reshape(1, -1), wg_b, wu_b, wd_b, norm_final_g.reshape(1, -1),
                 final_norm=last, tm=1024, tf=512)
    return h.reshape(batch, seq, d_model)
```

```python
import functools

import jax
import jax.numpy as jnp
from jax import lax
from jax.experimental import pallas as pl
from jax.experimental.pallas import tpu as pltpu

F32 = jnp.float32
BF16 = jnp.bfloat16

EPS = 1e-5
LOG2E = 1.4426950408889634
SSM_HEADDIM = 64
SSM_GROUPS = 8
SSM_STATE = 128
SSM_CONV = 4
SHORT_CONV = 3
CHUNK = 128
SUBLANES = 8
LANES = 128
BF16_ROWS = 16
PHASES = 4
VMEM_LIMIT = 60 * 1024 * 1024


def _cparams(sem):
    return pltpu.CompilerParams(dimension_semantics=sem, vmem_limit_bytes=VMEM_LIMIT)


def _rms_scale(x):
    return lax.rsqrt(jnp.mean(x * x, axis=-1, keepdims=True) + EPS)


def _silu_from_half(h):
    return h + h * jnp.tanh(h)


def _silu(x):
    return _silu_from_half(0.5 * x)


def _split3(v):
    hi = v.astype(BF16)
    r1 = v - hi.astype(F32)
    mid = r1.astype(BF16)
    lo = (r1 - mid.astype(F32)).astype(BF16)
    return hi, mid, lo


def _prep_kernel(blk_ref, nxt_ref, w_ref, wdt_ref, *, nt, shift):
    k = pl.program_id(0)
    tn = blk_ref.shape[0]

    @pl.when(k < nt)
    def _():
        w_ref[...] = blk_ref[...].astype(BF16)

    @pl.when(k >= nt)
    def _():
        w_ref[:tn - shift, :] = blk_ref[shift:, :].astype(BF16)
        w_ref[tn - shift:, :] = nxt_ref[...].astype(BF16)

    @pl.when(k == nt)
    def _():
        wdt_ref[:shift, :] = blk_ref[:shift, :].astype(BF16)
        wdt_ref[shift:, :] = jnp.zeros((LANES - shift, wdt_ref.shape[1]), BF16)


def _prep_w_in(wt, off_b, shift, tn):
    D = wt.shape[1]
    nt = off_b // tn
    assert off_b % tn == 0 and tn % shift == 0 and shift % BF16_ROWS == 0 and wt.shape[0] == 2 * off_b + shift
    return pl.pallas_call(
        functools.partial(_prep_kernel, nt=nt, shift=shift),
        grid=(2 * nt,),
        in_specs=[
            pl.BlockSpec((tn, D), lambda k: (k, 0)),
            pl.BlockSpec((shift, D), lambda k: ((k + 1) * (tn // shift), 0)),
        ],
        out_specs=[
            pl.BlockSpec((tn, D), lambda k: (k, 0)),
            pl.BlockSpec((LANES, D), lambda k: (0, 0)),
        ],
        out_shape=[jax.ShapeDtypeStruct((2 * off_b, D), BF16), jax.ShapeDtypeStruct((LANES, D), BF16)],
        compiler_params=_cparams(("arbitrary",)),
        name="prep_w_in",
    )(wt, wt)


def _in_proj_kernel(x_ref, g_ref, w_ref, wdt_ref, out_ref, dt_ref, n_scr, *, ncol):
    def mm(lhs, wt):
        return lax.dot_general(lhs, wt, (((1,), (1,)), ((), ())), preferred_element_type=F32)

    @pl.when(pl.program_id(1) == 0)
    def _():
        x = x_ref[...]
        n = (x * _rms_scale(x) * g_ref[...]).astype(BF16)
        n_scr[...] = n
        dt_ref[...] = mm(n, wdt_ref[...])

    for s in range(out_ref.shape[1] // ncol):
        cols = slice(s * ncol, (s + 1) * ncol)
        out_ref[:, cols] = mm(n_scr[...], w_ref[cols, :]).astype(BF16)


def _in_proj(x, g, w, w_dt, tm, tn):
    T, D = x.shape
    N = w.shape[0]
    return pl.pallas_call(
        functools.partial(_in_proj_kernel, ncol=min(tn, 1024)),
        grid=(T // tm, N // tn),
        in_specs=[
            pl.BlockSpec((tm, D), lambda i, j: (i, 0)),
            pl.BlockSpec((1, D), lambda i, j: (0, 0)),
            pl.BlockSpec((tn, D), lambda i, j: (j, 0)),
            pl.BlockSpec((LANES, D), lambda i, j: (0, 0)),
        ],
        out_specs=[
            pl.BlockSpec((tm, tn), lambda i, j: (i, j)),
            pl.BlockSpec((tm, LANES), lambda i, j: (i, 0)),
        ],
        out_shape=[jax.ShapeDtypeStruct((T, N), BF16), jax.ShapeDtypeStruct((T, LANES), F32)],
        scratch_shapes=[pltpu.VMEM((tm, D), BF16)],
        compiler_params=_cparams(("parallel", "arbitrary")),
        name="in_proj",
    )(x, g, w, w_dt)


def _slab_rows(rows, steps):
    per = -(-rows // steps)
    per = -(-per // BF16_ROWS) * BF16_ROWS
    while rows % per:
        per += BF16_ROWS
    return per


def _conv_phases(buf, c, w, width, rows):
    n = rows // PHASES
    shifted = {m: buf[c, pl.ds(SUBLANES + m, n, stride=PHASES), :] for m in range(1 - width, PHASES)}
    out = []
    for p in range(PHASES):
        acc = None
        for k in range(width):
            term = w[k:k + 1, :] * shifted[p - (width - 1) + k]
            acc = term if acc is None else acc + term
        out.append(acc)
    return out


def _store_phases(dst, c, phases, rows):
    for p, v in enumerate(phases):
        dst[c, pl.ds(p, rows // PHASES, stride=PHASES), :] = v


def _mixer_kernel(*refs, heads, nside):
    (z_ref, xs_ref, bc_ref, gb_ref, gc_ref, u_ref, dt_ref, cw_ref, cb_ref, scw_ref, dtb_ref, alog_ref, dexp_ref,
     ng_ref) = refs[:14]
    side_in = refs[14:14 + nside]
    y_ref, ysc_ref = refs[14 + nside:16 + nside]
    side_out = refs[16 + nside:16 + 2 * nside]
    buf, act, pbuf, pact, state, sel = refs[16 + 2 * nside:]
    L = CHUNK
    for src, dst in zip(side_in, side_out):
        dst[...] = src[...].astype(BF16)
    G, N, P = SSM_GROUPS, SSM_STATE, SSM_HEADDIM
    R = heads // G
    GW = R * P
    nx = heads * P // LANES
    nsc = pbuf.shape[0]
    lane = lambda c: slice(c * LANES, (c + 1) * LANES)

    @pl.when(pl.program_id(1) == 0)
    def _():
        buf[:, 0:SUBLANES, :] = jnp.zeros((buf.shape[0], SUBLANES, LANES), F32)
        pbuf[:, 0:SUBLANES, :] = jnp.zeros((nsc, SUBLANES, LANES), F32)
        state[...] = jnp.zeros(state.shape, F32)
        src_lane = lax.broadcasted_iota(jnp.int32, (LANES, 2 * heads * P), 0)
        out_col = lax.broadcasted_iota(jnp.int32, (LANES, 2 * heads * P), 1)
        grp = out_col // (2 * GW)
        kind = (out_col // GW) % 2
        head = grp * R + (out_col % GW) // P
        one = (src_lane == heads * (1 + kind) + head).astype(BF16)
        for k in range(3):
            sel[k * LANES:(k + 1) * LANES, :] = one

    def chunk(ci, carry):
        rows = pl.ds(pl.multiple_of(ci * L, L), L)

        dt_in = (dt_ref[rows, :] + dtb_ref[...]).T[:heads]
        dt = jnp.maximum(dt_in, 0.0) + jnp.log1p(jnp.exp(-jnp.abs(dt_in)))
        a2 = jnp.broadcast_to(-LOG2E * jnp.exp(alog_ref[...]), (L, LANES)).T[:heads]
        dA2 = dt * a2
        row = lax.broadcasted_iota(jnp.int32, (L, L), 0)
        col = lax.broadcasted_iota(jnp.int32, (L, L), 1)
        causal = row >= col
        triu = (row <= col).astype(BF16)
        cs2 = jnp.dot(jnp.concatenate(_split3(dA2), axis=1), jnp.concatenate([triu, triu, triu], axis=0),
                      preferred_element_type=F32)
        cs2_last = cs2[:, L - 1:L]
        ecs = jnp.exp2(cs2)
        wdec = dt * jnp.exp2(cs2_last - cs2)
        csd = cs2 - jnp.log2(dt)
        colT = jnp.concatenate([cs2, wdec, ecs, jnp.zeros((LANES - 3 * heads, L), F32)], axis=0).T
        colT3 = jnp.concatenate(_split3(colT), axis=1)

        for c in range(buf.shape[0]):
            src = xs_ref[rows, lane(c)] if c < nx else bc_ref[rows, lane(c - nx)]
            buf[c, SUBLANES:, :] = src.astype(F32)
        for c in range(buf.shape[0]):
            halves = _conv_phases(buf, c, 0.5 * cw_ref[:, lane(c)], SSM_CONV, L)
            b = 0.5 * cb_ref[:, lane(c)]
            _store_phases(act, c, [_silu_from_half(v + b) for v in halves], L)
            buf[c, 0:SUBLANES, :] = buf[c, L:L + SUBLANES, :]

        head_of_lane = lax.broadcasted_iota(jnp.int32, (L, GW), 1) // P
        ygs = []
        ssq = jnp.zeros((L, 1), F32)
        for g in range(G):
            sl = slice(g * GW, (g + 1) * GW)
            xs = jnp.concatenate([act[g * GW // LANES + k] for k in range(GW // LANES)], axis=1)
            xs_b = xs.astype(BF16)
            Bg = act[nx + g].astype(BF16)
            Cg = act[nx + G + g].astype(BF16)
            CB = lax.dot_general(Cg, Bg, (((1,), (1,)), ((), ())), preferred_element_type=F32)
            Hs = state[g]
            y_off = jnp.dot(Cg, Hs.astype(BF16), preferred_element_type=F32)
            Ms, xbd = [], []
            for r in range(R):
                h = g * R + r
                seg = colT[:, h:h + 1] - csd[h:h + 1, :]
                Ms.append((CB * jnp.exp2(jnp.where(causal, seg, -jnp.inf))).astype(BF16))
                xbd.append(jnp.where(head_of_lane == r, xs_b, jnp.zeros_like(xs_b)))
            y_diag = jnp.dot(jnp.concatenate(Ms, axis=1), jnp.concatenate(xbd, axis=0),
                             preferred_element_type=F32)
            expd = jnp.dot(colT3, sel[:, 2 * g * GW:2 * (g + 1) * GW], preferred_element_type=F32)
            wdec_e = expd[:, :GW]
            ecs_e = expd[:, GW:]
            xt_b = (xs * wdec_e).astype(BF16)
            st_new = lax.dot_general(Bg, xt_b, (((0,), (0,)), ((), ())), preferred_element_type=F32)
            state[g] = Hs * ecs_e[L - 1:L, :] + st_new
            y = y_diag + y_off * ecs_e + dexp_ref[:, sl] * xs
            yg = y * _silu(z_ref[rows, sl].astype(F32))
            ssq = ssq + jnp.sum(yg * yg, axis=-1, keepdims=True)
            ygs.append(yg)
        scale = lax.rsqrt(ssq * (1.0 / (G * GW)) + EPS)
        for g in range(G):
            sl = slice(g * GW, (g + 1) * GW)
            y_ref[rows, sl] = (ygs[g] * scale * ng_ref[:, sl]).astype(BF16)

        for c in range(nsc):
            pbuf[c, SUBLANES:, :] = gc_ref[rows, lane(c)].astype(F32) * u_ref[rows, lane(c)].astype(F32)
        for c in range(nsc):
            _store_phases(pact, c, _conv_phases(pbuf, c, scw_ref[:, lane(c)], SHORT_CONV, L), L)
            pbuf[c, 0:SUBLANES, :] = pbuf[c, L:L + SUBLANES, :]
        for c in range(nsc):
            ysc_ref[rows, lane(c)] = (gb_ref[rows, lane(c)].astype(F32) * pact[c]).astype(BF16)
        return carry

    lax.fori_loop(0, z_ref.shape[0] // L, chunk, 0)


def _mixer(proj, dt_raw, conv_w, conv_b, sc_conv_w, dt_bias, a_log, d_exp, norm_g, side, batch, seq, heads, tm):
    T = proj.shape[0]
    nt = seq // tm
    G, N = SSM_GROUPS, SSM_STATE
    d_ssm = heads * SSM_HEADDIM
    d_xbc = conv_w.shape[1]
    d_conv = sc_conv_w.shape[1]
    assert d_xbc - d_ssm == 2 * G * N == d_ssm == d_conv and tm % CHUNK == 0 and seq % tm == 0
    blk = lambda k: pl.BlockSpec((tm, d_ssm), lambda b, t: (b * nt + t, k))
    const = lambda b, t: (0, 0)
    side_specs = []
    for s in side:
        rb = _slab_rows(s.shape[0], batch * nt)
        last = s.shape[0] // rb - 1
        side_specs.append(pl.BlockSpec((rb, s.shape[1]),
                                       lambda b, t, last=last: (jnp.minimum(b * nt + t, last), 0)))
    outs = pl.pallas_call(
        functools.partial(_mixer_kernel, heads=heads, nside=len(side)),
        grid=(batch, nt),
        in_specs=[
            blk(0), blk(1), blk(2), blk(3), blk(4), blk(5),
            pl.BlockSpec((tm, LANES), lambda b, t: (b * nt + t, 0)),
            pl.BlockSpec((SSM_CONV, d_xbc), const),
            pl.BlockSpec((1, d_xbc), const),
            pl.BlockSpec((SHORT_CONV, d_conv), const),
            pl.BlockSpec((1, LANES), const),
            pl.BlockSpec((1, LANES), const),
            pl.BlockSpec((1, d_ssm), const),
            pl.BlockSpec((1, d_ssm), const),
        ] + side_specs,
        out_specs=[blk(0), blk(0)] + side_specs,
        out_shape=[jax.ShapeDtypeStruct((T, d_ssm), BF16), jax.ShapeDtypeStruct((T, d_conv), BF16)]
        + [jax.ShapeDtypeStruct(s.shape, BF16) for s in side],
        scratch_shapes=[pltpu.VMEM((d_xbc // LANES, SUBLANES + CHUNK, LANES), F32),
                        pltpu.VMEM((d_xbc // LANES, CHUNK, LANES), F32),
                        pltpu.VMEM((d_conv // LANES, SUBLANES + CHUNK, LANES), F32),
                        pltpu.VMEM((d_conv // LANES, CHUNK, LANES), F32),
                        pltpu.VMEM((G, N, d_ssm // G), F32),
                        pltpu.VMEM((3 * LANES, 2 * d_ssm), BF16)],
        compiler_params=_cparams(("arbitrary", "arbitrary")),
        name="mixer",
    )(proj, proj, proj, proj, proj, proj, dt_raw, conv_w, conv_b, sc_conv_w, dt_bias, a_log, d_exp, norm_g, *side)
    return outs[0], outs[1], outs[2:]


def _out_proj_kernel(ya_ref, yb_ref, w_ref, x_ref, h_ref, *, tn):
    ya = ya_ref[...]
    yb = yb_ref[...]
    Ka = ya.shape[1]
    for s in range(h_ref.shape[1] // tn):
        sl = slice(s * tn, (s + 1) * tn)
        acc = jnp.dot(ya, w_ref[:Ka, sl], preferred_element_type=F32)
        acc = acc + jnp.dot(yb, w_ref[Ka:, sl], preferred_element_type=F32)
        h_ref[:, sl] = x_ref[:, sl] + acc


def _out_proj(y_ssm, y_sc, w_out, x, tm, tn):
    T, D = x.shape
    Ka = y_ssm.shape[1]
    Kb = y_sc.shape[1]
    return pl.pallas_call(
        functools.partial(_out_proj_kernel, tn=tn),
        grid=(T // tm,),
        in_specs=[
            pl.BlockSpec((tm, Ka), lambda i: (i, 0)),
            pl.BlockSpec((tm, Kb), lambda i: (i, 0)),
            pl.BlockSpec((Ka + Kb, D), lambda i: (0, 0), pipeline_mode=pl.Buffered(1)),
            pl.BlockSpec((tm, D), lambda i: (i, 0)),
        ],
        out_specs=pl.BlockSpec((tm, D), lambda i: (i, 0)),
        out_shape=jax.ShapeDtypeStruct((T, D), F32),
        compiler_params=_cparams(("parallel",)),
        name="out_proj",
    )(y_ssm, y_sc, w_out, x)


def _ffn_kernel(h_ref, g_ref, wg_ref, wu_ref, wd_ref, gf_ref, o_ref, n_scr, *, final_norm, nsplit):
    f = pl.program_id(1)

    @pl.when(f == 0)
    def _():
        h = h_ref[...]
        n_scr[...] = (h * _rms_scale(h) * g_ref[...]).astype(BF16)
        o_ref[...] = h

    n = n_scr[...]
    gate = jnp.dot(n, wg_ref[...], preferred_element_type=F32)
    up = jnp.dot(n, wu_ref[...], preferred_element_type=F32)
    a = (_silu(gate) * up).astype(BF16)
    wn = o_ref.shape[1] // nsplit
    for s in range(nsplit):
        sl = slice(s * wn, (s + 1) * wn)
        o_ref[:, sl] += jnp.dot(a, wd_ref[:, sl], preferred_element_type=F32)

    if final_norm:
        @pl.when(f == pl.num_programs(1) - 1)
        def _():
            h2 = o_ref[...]
            o_ref[...] = h2 * _rms_scale(h2) * gf_ref[...]


def _ffn(h1, g, w_gate, w_up, w_down, g_final, final_norm, tm, tf):
    T, D = h1.shape
    F = w_gate.shape[1]
    return pl.pallas_call(
        functools.partial(_ffn_kernel, final_norm=final_norm, nsplit=4),
        grid=(T // tm, F // tf),
        in_specs=[
            pl.BlockSpec((tm, D), lambda i, f: (i, 0)),
            pl.BlockSpec((1, D), lambda i, f: (0, 0)),
            pl.BlockSpec((D, tf), lambda i, f: (0, f)),
            pl.BlockSpec((D, tf), lambda i, f: (0, f)),
            pl.BlockSpec((tf, D), lambda i, f: (f, 0)),
            pl.BlockSpec((1, D), lambda i, f: (0, 0)),
        ],
        out_specs=pl.BlockSpec((tm, D), lambda i, f: (i, 0)),
        out_shape=jax.ShapeDtypeStruct((T, D), F32),
        scratch_shapes=[pltpu.VMEM((tm, D), BF16)],
        compiler_params=_cparams(("parallel", "arbitrary")),
        name="ffn",
    )(h1, g, w_gate, w_up, w_down, g_final)


def _pad_lanes(v):
    return jnp.pad(v.reshape(1, -1), ((0, 0), (0, LANES - v.shape[-1])))


def kernel(x, norm_mix_g, w_in, ssm_conv_w, ssm_conv_b, ssm_dt_bias, ssm_A_log, ssm_D, ssm_norm_g,
           sc_conv_w, w_out, norm_ffn_g, w_gate, w_up, w_down, norm_final_g):
    batch, seq, d_model = x.shape
    depth = w_in.shape[0]
    d_ssm = ssm_norm_g.shape[1]
    d_xbc = ssm_conv_w.shape[2]
    heads = ssm_dt_bias.shape[1]
    d_conv = sc_conv_w.shape[2]
    assert d_ssm == d_conv == d_model and heads * SSM_HEADDIM == d_ssm and 3 * heads <= LANES
    assert seq % 1024 == 0
    off_dt = d_ssm + d_xbc
    off_cb = off_dt + heads
    assert w_in.shape[2] - off_cb == off_dt

    h = x.reshape(batch * seq, d_model)
    for l in range(depth):
        w_proj, w_dt = _prep_w_in(w_in[l].T, off_dt, heads, tn=512)
        proj, dt_raw = _in_proj(h, norm_mix_g[l].reshape(1, -1), w_proj, w_dt, tm=1024, tn=2048)
        y_ssm, y_sc, (wo_b, wg_b, wu_b, wd_b) = _mixer(
            proj, dt_raw, ssm_conv_w[l], ssm_conv_b[l].reshape(1, -1), sc_conv_w[l], _pad_lanes(ssm_dt_bias[l]),
            _pad_lanes(ssm_A_log[l]), jnp.repeat(ssm_D[l], SSM_HEADDIM).reshape(1, -1),
            ssm_norm_g[l].reshape(1, -1), (w_out[l], w_gate[l], w_up[l], w_down[l]), batch, seq, heads, tm=256)
        h1 = _out_proj(y_ssm, y_sc, wo_b, h, tm=512, tn=512)
        last = l == depth - 1
        h = _ffn(h1, norm_ffn_g[l].reshape(1, -1), wg_b, wu_b, wd_b, norm_final_g.reshape(1, -1),
                 final_norm=last, tm=1024, tf=512)
    return h.reshape(batch, seq, d_model)
```

```python
import functools

import jax
import jax.numpy as jnp
from jax import lax
from jax.experimental import pallas as pl
from jax.experimental.pallas import tpu as pltpu

F32 = jnp.float32
BF16 = jnp.bfloat16

EPS = 1e-5
LOG2E = 1.4426950408889634
SSM_HEADDIM = 64
SSM_GROUPS = 8
SSM_STATE = 128
SSM_CONV = 4
SHORT_CONV = 3
CHUNK = 128
SUBLANES = 8
LANES = 128
BF16_ROWS = 16
PHASES = 4
VMEM_LIMIT = 60 * 1024 * 1024


def _cparams(sem):
    return pltpu.CompilerParams(dimension_semantics=sem, vmem_limit_bytes=VMEM_LIMIT)


def _rms_scale(x):
    return lax.rsqrt(jnp.mean(x * x, axis=-1, keepdims=True) + EPS)


def _silu_from_half(h):
    return h + h * jnp.tanh(h)


def _silu(x):
    return _silu_from_half(0.5 * x)


def _split3(v):
    hi = v.astype(BF16)
    r1 = v - hi.astype(F32)
    mid = r1.astype(BF16)
    lo = (r1 - mid.astype(F32)).astype(BF16)
    return hi, mid, lo


def _prep_kernel(blk_ref, nxt_ref, w_ref, wdt_ref, *, nt, shift):
    k = pl.program_id(0)
    tn = blk_ref.shape[0]

    @pl.when(k < nt)
    def _():
        w_ref[...] = blk_ref[...].astype(BF16)

    @pl.when(k >= nt)
    def _():
        w_ref[:tn - shift, :] = blk_ref[shift:, :].astype(BF16)
        w_ref[tn - shift:, :] = nxt_ref[...].astype(BF16)

    @pl.when(k == nt)
    def _():
        wdt_ref[:shift, :] = blk_ref[:shift, :].astype(BF16)
        wdt_ref[shift:, :] = jnp.zeros((LANES - shift, wdt_ref.shape[1]), BF16)


def _prep_w_in(wt, off_b, shift, tn):
    D = wt.shape[1]
    nt = off_b // tn
    assert off_b % tn == 0 and tn % shift == 0 and shift % BF16_ROWS == 0 and wt.shape[0] == 2 * off_b + shift
    return pl.pallas_call(
        functools.partial(_prep_kernel, nt=nt, shift=shift),
        grid=(2 * nt,),
        in_specs=[
            pl.BlockSpec((tn, D), lambda k: (k, 0)),
            pl.BlockSpec((shift, D), lambda k: ((k + 1) * (tn // shift), 0)),
        ],
        out_specs=[
            pl.BlockSpec((tn, D), lambda k: (k, 0)),
            pl.BlockSpec((LANES, D), lambda k: (0, 0)),
        ],
        out_shape=[jax.ShapeDtypeStruct((2 * off_b, D), BF16), jax.ShapeDtypeStruct((LANES, D), BF16)],
        compiler_params=_cparams(("arbitrary",)),
        name="prep_w_in",
    )(wt, wt)


def _in_proj_kernel(x_ref, g_ref, w_ref, wdt_ref, out_ref, dt_ref, n_scr, *, ncol):
    def mm(lhs, wt):
        return lax.dot_general(lhs, wt, (((1,), (1,)), ((), ())), preferred_element_type=F32)

    @pl.when(pl.program_id(1) == 0)
    def _():
        x = x_ref[...]
        n = (x * _rms_scale(x) * g_ref[...]).astype(BF16)
        n_scr[...] = n
        dt_ref[...] = mm(n, wdt_ref[...])

    for s in range(out_ref.shape[1] // ncol):
        cols = slice(s * ncol, (s + 1) * ncol)
        out_ref[:, cols] = mm(n_scr[...], w_ref[cols, :]).astype(BF16)


def _in_proj(x, g, w, w_dt, tm, tn):
    T, D = x.shape
    N = w.shape[0]
    return pl.pallas_call(
        functools.partial(_in_proj_kernel, ncol=min(tn, 1024)),
        grid=(T // tm, N // tn),
        in_specs=[
            pl.BlockSpec((tm, D), lambda i, j: (i, 0)),
            pl.BlockSpec((1, D), lambda i, j: (0, 0)),
            pl.BlockSpec((tn, D), lambda i, j: (j, 0)),
            pl.BlockSpec((LANES, D), lambda i, j: (0, 0)),
        ],
        out_specs=[
            pl.BlockSpec((tm, tn), lambda i, j: (i, j)),
            pl.BlockSpec((tm, LANES), lambda i, j: (i, 0)),
        ],
        out_shape=[jax.ShapeDtypeStruct((T, N), BF16), jax.ShapeDtypeStruct((T, LANES), F32)],
        scratch_shapes=[pltpu.VMEM((tm, D), BF16)],
        compiler_params=_cparams(("parallel", "arbitrary")),
        name="in_proj",
    )(x, g, w, w_dt)


def _slab_rows(rows, steps):
    per = -(-rows // steps)
    per = -(-per // BF16_ROWS) * BF16_ROWS
    while rows % per:
        per += BF16_ROWS
    return per


def _conv_phases(buf, c, w, width, rows):
    n = rows // PHASES
    shifted = {m: buf[c, pl.ds(SUBLANES + m, n, stride=PHASES), :] for m in range(1 - width, PHASES)}
    out = []
    for p in range(PHASES):
        acc = None
        for k in range(width):
            term = w[k:k + 1, :] * shifted[p - (width - 1) + k]
            acc = term if acc is None else acc + term
        out.append(acc)
    return out


def _store_phases(dst, c, phases, rows):
    for p, v in enumerate(phases):
        dst[c, pl.ds(p, rows // PHASES, stride=PHASES), :] = v


def _mixer_kernel(*refs, heads, nside):
    xs_ref, bc_ref, dt_ref, cw_ref, cb_ref, dtb_ref, alog_ref, dexp_ref = refs[:8]
    side_in = refs[8:8 + nside]
    y_ref = refs[8 + nside]
    side_out = refs[9 + nside:9 + 2 * nside]
    buf, act, state, sel = refs[9 + 2 * nside:]
    L = CHUNK
    for src, dst in zip(side_in, side_out):
        dst[...] = src[...].astype(BF16)
    G, N, P = SSM_GROUPS, SSM_STATE, SSM_HEADDIM
    R = heads // G
    GW = R * P
    nx = heads * P // LANES
    lane = lambda c: slice(c * LANES, (c + 1) * LANES)

    @pl.when(pl.program_id(1) == 0)
    def _():
        buf[:, 0:SUBLANES, :] = jnp.zeros((buf.shape[0], SUBLANES, LANES), F32)
        state[...] = jnp.zeros(state.shape, F32)
        src_lane = lax.broadcasted_iota(jnp.int32, (LANES, 2 * heads * P), 0)
        out_col = lax.broadcasted_iota(jnp.int32, (LANES, 2 * heads * P), 1)
        grp = out_col // (2 * GW)
        kind = (out_col // GW) % 2
        head = grp * R + (out_col % GW) // P
        one = (src_lane == heads * (1 + kind) + head).astype(BF16)
        for k in range(3):
            sel[k * LANES:(k + 1) * LANES, :] = one

    def chunk(ci, carry):
        rows = pl.ds(pl.multiple_of(ci * L, L), L)

        dt_in = (dt_ref[rows, :] + dtb_ref[...]).T[:heads]
        dt = jnp.maximum(dt_in, 0.0) + jnp.log1p(jnp.exp(-jnp.abs(dt_in)))
        a2 = jnp.broadcast_to(-LOG2E * jnp.exp(alog_ref[...]), (L, LANES)).T[:heads]
        dA2 = dt * a2
        row = lax.broadcasted_iota(jnp.int32, (L, L), 0)
        col = lax.broadcasted_iota(jnp.int32, (L, L), 1)
        causal = row >= col
        triu = (row <= col).astype(BF16)
        cs2 = jnp.dot(jnp.concatenate(_split3(dA2), axis=1), jnp.concatenate([triu, triu, triu], axis=0),
                      preferred_element_type=F32)
        cs2_last = cs2[:, L - 1:L]
        ecs = jnp.exp2(cs2)
        wdec = dt * jnp.exp2(cs2_last - cs2)
        csd = cs2 - jnp.log2(dt)
        colT = jnp.concatenate([cs2, wdec, ecs, jnp.zeros((LANES - 3 * heads, L), F32)], axis=0).T
        colT3 = jnp.concatenate(_split3(colT), axis=1)

        for c in range(buf.shape[0]):
            src = xs_ref[rows, lane(c)] if c < nx else bc_ref[rows, lane(c - nx)]
            buf[c, SUBLANES:, :] = src.astype(F32)
        for c in range(buf.shape[0]):
            halves = _conv_phases(buf, c, 0.5 * cw_ref[:, lane(c)], SSM_CONV, L)
            b = 0.5 * cb_ref[:, lane(c)]
            _store_phases(act, c, [_silu_from_half(v + b) for v in halves], L)
            buf[c, 0:SUBLANES, :] = buf[c, L:L + SUBLANES, :]

        head_of_lane = lax.broadcasted_iota(jnp.int32, (L, GW), 1) // P
        for g in range(G):
            sl = slice(g * GW, (g + 1) * GW)
            xs = jnp.concatenate([act[g * GW // LANES + k] for k in range(GW // LANES)], axis=1)
            xs_b = xs.astype(BF16)
            Bg = act[nx + g].astype(BF16)
            Cg = act[nx + G + g].astype(BF16)
            CB = lax.dot_general(Cg, Bg, (((1,), (1,)), ((), ())), preferred_element_type=F32)
            Hs = state[g]
            y_off = jnp.dot(Cg, Hs.astype(BF16), preferred_element_type=F32)
            Ms, xbd = [], []
            for r in range(R):
                h = g * R + r
                seg = colT[:, h:h + 1] - csd[h:h + 1, :]
                Ms.append((CB * jnp.exp2(jnp.where(causal, seg, -jnp.inf))).astype(BF16))
                xbd.append(jnp.where(head_of_lane == r, xs_b, jnp.zeros_like(xs_b)))
            y_diag = jnp.dot(jnp.concatenate(Ms, axis=1), jnp.concatenate(xbd, axis=0),
                             preferred_element_type=F32)
            expd = jnp.dot(colT3, sel[:, 2 * g * GW:2 * (g + 1) * GW], preferred_element_type=F32)
            wdec_e = expd[:, :GW]
            ecs_e = expd[:, GW:]
            xt_b = (xs * wdec_e).astype(BF16)
            st_new = lax.dot_general(Bg, xt_b, (((0,), (0,)), ((), ())), preferred_element_type=F32)
            state[g] = Hs * ecs_e[L - 1:L, :] + st_new
            y_ref[rows, sl] = (y_diag + y_off * ecs_e + dexp_ref[:, sl] * xs).astype(BF16)
        return carry

    lax.fori_loop(0, xs_ref.shape[0] // L, chunk, 0)


def _mixer(proj, dt_raw, conv_w, conv_b, dt_bias, a_log, d_exp, side, batch, seq, heads, tm):
    T = proj.shape[0]
    nt = seq // tm
    G, N = SSM_GROUPS, SSM_STATE
    d_ssm = heads * SSM_HEADDIM
    d_xbc = conv_w.shape[1]
    assert d_xbc - d_ssm == 2 * G * N == d_ssm and tm % CHUNK == 0 and seq % tm == 0
    blk = lambda k: pl.BlockSpec((tm, d_ssm), lambda b, t: (b * nt + t, k))
    const = lambda b, t: (0, 0)
    side_specs = []
    for s in side:
        rb = _slab_rows(s.shape[0], batch * nt)
        last = s.shape[0] // rb - 1
        side_specs.append(pl.BlockSpec((rb, s.shape[1]),
                                       lambda b, t, last=last: (jnp.minimum(b * nt + t, last), 0)))
    outs = pl.pallas_call(
        functools.partial(_mixer_kernel, heads=heads, nside=len(side)),
        grid=(batch, nt),
        in_specs=[
            blk(1), blk(2),
            pl.BlockSpec((tm, LANES), lambda b, t: (b * nt + t, 0)),
            pl.BlockSpec((SSM_CONV, d_xbc), const),
            pl.BlockSpec((1, d_xbc), const),
            pl.BlockSpec((1, LANES), const),
            pl.BlockSpec((1, LANES), const),
            pl.BlockSpec((1, d_ssm), const),
        ] + side_specs,
        out_specs=[blk(0)] + side_specs,
        out_shape=[jax.ShapeDtypeStruct((T, d_ssm), BF16)] + [jax.ShapeDtypeStruct(s.shape, BF16) for s in side],
        scratch_shapes=[pltpu.VMEM((d_xbc // LANES, SUBLANES + CHUNK, LANES), F32),
                        pltpu.VMEM((d_xbc // LANES, CHUNK, LANES), F32),
                        pltpu.VMEM((G, N, d_ssm // G), F32),
                        pltpu.VMEM((3 * LANES, 2 * d_ssm), BF16)],
        compiler_params=_cparams(("arbitrary", "arbitrary")),
        name="mixer",
    )(proj, proj, dt_raw, conv_w, conv_b, dt_bias, a_log, d_exp, *side)
    return outs[0], outs[1:]


def _causal_conv_rows(u, prev, w, width):
    rows, C = u.shape
    nb = rows // SUBLANES
    full = jnp.concatenate([prev, u], axis=0).reshape(nb + 1, SUBLANES, C)
    sub = lax.broadcasted_iota(jnp.int32, (nb, SUBLANES, C), 1)
    tap = lambda k: jnp.broadcast_to(w[k:k + 1, :], (SUBLANES, C)).reshape(1, SUBLANES, C)
    y = full[1:] * tap(width - 1)
    for k in range(width - 1):
        s = width - 1 - k
        r = pltpu.roll(full, s, 1)
        y = y + jnp.where(sub >= s, r[1:], r[:-1]) * tap(k)
    return y.reshape(rows, C)


def _out_proj_kernel(y_ref, z_ref, ng_ref, gb_ref, gc_ref, u_ref, scw_ref, w_ref, x_ref, h_ref, carry, *, tn,
                     tiles_per_seq):
    tm, Ka = y_ref.shape
    cols = [slice(s * tn, (s + 1) * tn) for s in range(h_ref.shape[1] // tn)]
    p = gc_ref[...].astype(F32) * u_ref[...].astype(F32)
    prev = jnp.where((pl.program_id(0) % tiles_per_seq) == 0, 0.0, carry[...])
    carry[...] = p[tm - SUBLANES:, :]
    yb = (gb_ref[...].astype(F32) * _causal_conv_rows(p, prev, scw_ref[...], SHORT_CONV)).astype(BF16)
    for sl in cols:
        h_ref[:, sl] = x_ref[:, sl] + jnp.dot(yb, w_ref[Ka:, sl], preferred_element_type=F32)
    yg = y_ref[...].astype(F32) * _silu(z_ref[...].astype(F32))
    ya = (yg * _rms_scale(yg) * ng_ref[...]).astype(BF16)
    for sl in cols:
        h_ref[:, sl] += jnp.dot(ya, w_ref[:Ka, sl], preferred_element_type=F32)


def _out_proj(y, proj, norm_g, col0, sc_conv_w, w_out, x, seq, tm, tn):
    T, D = x.shape
    Ka = y.shape[1]
    Kb = sc_conv_w.shape[1]
    assert seq % tm == 0
    blk = lambda k: pl.BlockSpec((tm, Kb), lambda i: (i, col0 + k))
    return pl.pallas_call(
        functools.partial(_out_proj_kernel, tn=tn, tiles_per_seq=seq // tm),
        grid=(T // tm,),
        in_specs=[
            pl.BlockSpec((tm, Ka), lambda i: (i, 0)),
            pl.BlockSpec((tm, Ka), lambda i: (i, 0)),
            pl.BlockSpec((1, Ka), lambda i: (0, 0)),
            blk(0), blk(1), blk(2),
            pl.BlockSpec((SHORT_CONV, Kb), lambda i: (0, 0)),
            pl.BlockSpec((Ka + Kb, D), lambda i: (0, 0), pipeline_mode=pl.Buffered(1)),
            pl.BlockSpec((tm, D), lambda i: (i, 0)),
        ],
        out_specs=pl.BlockSpec((tm, D), lambda i: (i, 0)),
        out_shape=jax.ShapeDtypeStruct((T, D), F32),
        scratch_shapes=[pltpu.VMEM((SUBLANES, Kb), F32)],
        compiler_params=_cparams(("arbitrary",)),
        name="out_proj",
    )(y, proj, norm_g, proj, proj, proj, sc_conv_w, w_out, x)


def _ffn_kernel(h_ref, g_ref, wg_ref, wu_ref, wd_ref, gf_ref, o_ref, n_scr, *, final_norm, nsplit):
    f = pl.program_id(1)

    @pl.when(f == 0)
    def _():
        h = h_ref[...]
        n_scr[...] = (h * _rms_scale(h) * g_ref[...]).astype(BF16)
        o_ref[...] = h

    n = n_scr[...]
    gate = jnp.dot(n, wg_ref[...], preferred_element_type=F32)
    up = jnp.dot(n, wu_ref[...], preferred_element_type=F32)
    a = (_silu(gate) * up).astype(BF16)
    wn = o_ref.shape[1] // nsplit
    for s in range(nsplit):
        sl = slice(s * wn, (s + 1) * wn)
        o_ref[:, sl] += jnp.dot(a, wd_ref[:, sl], preferred_element_type=F32)

    if final_norm:
        @pl.when(f == pl.num_programs(1) - 1)
        def _():
            h2 = o_ref[...]
            o_ref[...] = h2 * _rms_scale(h2) * gf_ref[...]


def _ffn(h1, g, w_gate, w_up, w_down, g_final, final_norm, tm, tf):
    T, D = h1.shape
    F = w_gate.shape[1]
    return pl.pallas_call(
        functools.partial(_ffn_kernel, final_norm=final_norm, nsplit=4),
        grid=(T // tm, F // tf),
        in_specs=[
            pl.BlockSpec((tm, D), lambda i, f: (i, 0)),
            pl.BlockSpec((1, D), lambda i, f: (0, 0)),
            pl.BlockSpec((D, tf), lambda i, f: (0, f)),
            pl.BlockSpec((D, tf), lambda i, f: (0, f)),
            pl.BlockSpec((tf, D), lambda i, f: (f, 0)),
            pl.BlockSpec((1, D), lambda i, f: (0, 0)),
        ],
        out_specs=pl.BlockSpec((tm, D), lambda i, f: (i, 0)),
        out_shape=jax.ShapeDtypeStruct((T, D), F32),
        scratch_shapes=[pltpu.VMEM((tm, D), BF16)],
        compiler_params=_cparams(("parallel", "arbitrary")),
        name="ffn",
    )(h1, g, w_gate, w_up, w_down, g_final)


def _pad_lanes(v):
    return jnp.pad(v.reshape(1, -1), ((0, 0), (0, LANES - v.shape[-1])))


def kernel(x, norm_mix_g, w_in, ssm_conv_w, ssm_conv_b, ssm_dt_bias, ssm_A_log, ssm_D, ssm_norm_g,
           sc_conv_w, w_out, norm_ffn_g, w_gate, w_up, w_down, norm_final_g):
    batch, seq, d_model = x.shape
    depth = w_in.shape[0]
    d_ssm = ssm_norm_g.shape[1]
    d_xbc = ssm_conv_w.shape[2]
    heads = ssm_dt_bias.shape[1]
    d_conv = sc_conv_w.shape[2]
    assert d_ssm == d_conv == d_model and heads * SSM_HEADDIM == d_ssm and 3 * heads <= LANES
    assert seq % 1024 == 0
    off_dt = d_ssm + d_xbc
    off_cb = off_dt + heads
    assert w_in.shape[2] - off_cb == off_dt

    h = x.reshape(batch * seq, d_model)
    for l in range(depth):
        w_proj, w_dt = _prep_w_in(w_in[l].T, off_dt, heads, tn=512)
        proj, dt_raw = _in_proj(h, norm_mix_g[l].reshape(1, -1), w_proj, w_dt, tm=1024, tn=2048)
        y, (wo_b, wg_b, wu_b, wd_b) = _mixer(
            proj, dt_raw, ssm_conv_w[l], ssm_conv_b[l].reshape(1, -1), _pad_lanes(ssm_dt_bias[l]),
            _pad_lanes(ssm_A_log[l]), jnp.repeat(ssm_D[l], SSM_HEADDIM).reshape(1, -1),
            (w_out[l], w_gate[l], w_up[l], w_down[l]), batch, seq, heads, tm=256)
        h1 = _out_proj(y, proj, ssm_norm_g[l].reshape(1, -1), (d_ssm + d_xbc) // d_conv, sc_conv_w[l], wo_b, h,
                       seq, tm=512, tn=512)
        last = l == depth - 1
        h = _ffn(h1, norm_ffn_g[l].reshape(1, -1), wg_b, wu_b, wd_b, norm_final_g.reshape(1, -1),
                 final_norm=last, tm=1024, tf=512)
    return h.reshape(batch, seq, d_model)
```

```python
import functools

import jax
import jax.numpy as jnp
from jax import lax
from jax.experimental import pallas as pl
from jax.experimental.pallas import tpu as pltpu

F32 = jnp.float32
BF16 = jnp.bfloat16

EPS = 1e-5
LOG2E = 1.4426950408889634
SSM_HEADDIM = 64
SSM_GROUPS = 8
SSM_STATE = 128
SSM_CONV = 4
SHORT_CONV = 3
CHUNK = 128
SUBLANES = 8
LANES = 128
BF16_ROWS = 16
PHASES = 4
SEL_TERMS = 2
VMEM_LIMIT = 60 * 1024 * 1024


def _cparams(sem):
    return pltpu.CompilerParams(dimension_semantics=sem, vmem_limit_bytes=VMEM_LIMIT)


def _rms_scale(x):
    return lax.rsqrt(jnp.mean(x * x, axis=-1, keepdims=True) + EPS)


def _silu_from_half(h):
    return h + h * jnp.tanh(h)


def _silu(x):
    return _silu_from_half(0.5 * x)


def _split3(v):
    hi = v.astype(BF16)
    r1 = v - hi.astype(F32)
    mid = r1.astype(BF16)
    lo = (r1 - mid.astype(F32)).astype(BF16)
    return hi, mid, lo


def _prep_kernel(blk_ref, nxt_ref, w_ref, wdt_ref, *, nt, shift):
    k = pl.program_id(0)
    tn = blk_ref.shape[0]

    @pl.when(k < nt)
    def _():
        w_ref[...] = blk_ref[...].astype(BF16)

    @pl.when(k >= nt)
    def _():
        w_ref[:tn - shift, :] = blk_ref[shift:, :].astype(BF16)
        w_ref[tn - shift:, :] = nxt_ref[...].astype(BF16)

    @pl.when(k == nt)
    def _():
        wdt_ref[:shift, :] = blk_ref[:shift, :].astype(BF16)
        wdt_ref[shift:, :] = jnp.zeros((LANES - shift, wdt_ref.shape[1]), BF16)


def _prep_w_in(wt, off_b, shift, tn):
    D = wt.shape[1]
    nt = off_b // tn
    assert off_b % tn == 0 and tn % shift == 0 and shift % BF16_ROWS == 0 and wt.shape[0] == 2 * off_b + shift
    return pl.pallas_call(
        functools.partial(_prep_kernel, nt=nt, shift=shift),
        grid=(2 * nt,),
        in_specs=[
            pl.BlockSpec((tn, D), lambda k: (k, 0)),
            pl.BlockSpec((shift, D), lambda k: ((k + 1) * (tn // shift), 0)),
        ],
        out_specs=[
            pl.BlockSpec((tn, D), lambda k: (k, 0)),
            pl.BlockSpec((LANES, D), lambda k: (0, 0)),
        ],
        out_shape=[jax.ShapeDtypeStruct((2 * off_b, D), BF16), jax.ShapeDtypeStruct((LANES, D), BF16)],
        compiler_params=_cparams(("arbitrary",)),
        name="prep_w_in",
    )(wt, wt)


def _in_proj_kernel(x_ref, g_ref, w_ref, wdt_ref, out_ref, dt_ref, n_scr, *, ncol):
    def mm(lhs, wt):
        return lax.dot_general(lhs, wt, (((1,), (1,)), ((), ())), preferred_element_type=F32)

    @pl.when(pl.program_id(1) == 0)
    def _():
        x = x_ref[...]
        n = (x * _rms_scale(x) * g_ref[...]).astype(BF16)
        n_scr[...] = n
        dt_ref[...] = mm(n, wdt_ref[...])

    for s in range(out_ref.shape[1] // ncol):
        cols = slice(s * ncol, (s + 1) * ncol)
        out_ref[:, cols] = mm(n_scr[...], w_ref[cols, :]).astype(BF16)


def _in_proj(x, g, w, w_dt, tm, tn):
    T, D = x.shape
    N = w.shape[0]
    return pl.pallas_call(
        functools.partial(_in_proj_kernel, ncol=min(tn, 1024)),
        grid=(T // tm, N // tn),
        in_specs=[
            pl.BlockSpec((tm, D), lambda i, j: (i, 0)),
            pl.BlockSpec((1, D), lambda i, j: (0, 0)),
            pl.BlockSpec((tn, D), lambda i, j: (j, 0)),
            pl.BlockSpec((LANES, D), lambda i, j: (0, 0)),
        ],
        out_specs=[
            pl.BlockSpec((tm, tn), lambda i, j: (i, j)),
            pl.BlockSpec((tm, LANES), lambda i, j: (i, 0)),
        ],
        out_shape=[jax.ShapeDtypeStruct((T, N), BF16), jax.ShapeDtypeStruct((T, LANES), F32)],
        scratch_shapes=[pltpu.VMEM((tm, D), BF16)],
        compiler_params=_cparams(("parallel", "arbitrary")),
        name="in_proj",
    )(x, g, w, w_dt)


def _slab_rows(rows, steps):
    per = -(-rows // steps)
    per = -(-per // BF16_ROWS) * BF16_ROWS
    while rows % per:
        per += BF16_ROWS
    return per


def _conv_phases(buf, c, w, width, rows):
    n = rows // PHASES
    shifted = {m: buf[c, pl.ds(SUBLANES + m, n, stride=PHASES), :] for m in range(1 - width, PHASES)}
    out = []
    for p in range(PHASES):
        acc = None
        for k in range(width):
            term = w[k:k + 1, :] * shifted[p - (width - 1) + k]
            acc = term if acc is None else acc + term
        out.append(acc)
    return out


def _store_phases(dst, c, phases, rows):
    for p, v in enumerate(phases):
        dst[c, pl.ds(p, rows // PHASES, stride=PHASES), :] = v


def _mixer_kernel(*refs, heads, nside):
    xs_ref, bc_ref, dt_ref, cw_ref, cb_ref, dtb_ref, alog_ref, dexp_ref = refs[:8]
    side_in = refs[8:8 + nside]
    y_ref = refs[8 + nside]
    side_out = refs[9 + nside:9 + 2 * nside]
    buf, act, state, sel = refs[9 + 2 * nside:]
    L = CHUNK
    for src, dst in zip(side_in, side_out):
        dst[...] = src[...].astype(BF16)
    G, N, P = SSM_GROUPS, SSM_STATE, SSM_HEADDIM
    R = heads // G
    GW = R * P
    nx = heads * P // LANES
    lane = lambda c: slice(c * LANES, (c + 1) * LANES)

    @pl.when(pl.program_id(1) == 0)
    def _():
        buf[:, 0:SUBLANES, :] = jnp.zeros((buf.shape[0], SUBLANES, LANES), F32)
        state[...] = jnp.zeros(state.shape, F32)
        src_lane = lax.broadcasted_iota(jnp.int32, (LANES, 2 * heads * P), 0)
        out_col = lax.broadcasted_iota(jnp.int32, (LANES, 2 * heads * P), 1)
        grp = out_col // (2 * GW)
        kind = (out_col // GW) % 2
        head = grp * R + (out_col % GW) // P
        one = (src_lane == heads * (1 + kind) + head).astype(BF16)
        for k in range(SEL_TERMS):
            sel[k * LANES:(k + 1) * LANES, :] = one

    def chunk(ci, carry):
        rows = pl.ds(pl.multiple_of(ci * L, L), L)

        dt_in = (dt_ref[rows, :] + dtb_ref[...]).T[:heads]
        dt = jnp.maximum(dt_in, 0.0) + jnp.log1p(jnp.exp(-jnp.abs(dt_in)))
        a2 = jnp.broadcast_to(-LOG2E * jnp.exp(alog_ref[...]), (L, LANES)).T[:heads]
        dA2 = dt * a2
        row = lax.broadcasted_iota(jnp.int32, (L, L), 0)
        col = lax.broadcasted_iota(jnp.int32, (L, L), 1)
        causal = row >= col
        triu = (row <= col).astype(BF16)
        cs2 = jnp.dot(jnp.concatenate(_split3(dA2), axis=1), jnp.concatenate([triu, triu, triu], axis=0),
                      preferred_element_type=F32)
        cs2_last = cs2[:, L - 1:L]
        ecs = jnp.exp2(cs2)
        wdec = dt * jnp.exp2(cs2_last - cs2)
        csd = cs2 - jnp.log2(dt)
        colT = jnp.concatenate([cs2, wdec, ecs, jnp.zeros((LANES - 3 * heads, L), F32)], axis=0).T
        colT3 = jnp.concatenate(_split3(colT)[:SEL_TERMS], axis=1)

        for c in range(buf.shape[0]):
            src = xs_ref[rows, lane(c)] if c < nx else bc_ref[rows, lane(c - nx)]
            buf[c, SUBLANES:, :] = src.astype(F32)
        for c in range(buf.shape[0]):
            halves = _conv_phases(buf, c, 0.5 * cw_ref[:, lane(c)], SSM_CONV, L)
            b = 0.5 * cb_ref[:, lane(c)]
            _store_phases(act, c, [_silu_from_half(v + b) for v in halves], L)
            buf[c, 0:SUBLANES, :] = buf[c, L:L + SUBLANES, :]

        head_of_lane = lax.broadcasted_iota(jnp.int32, (L, GW), 1) // P
        for g in range(G):
            sl = slice(g * GW, (g + 1) * GW)
            xs = jnp.concatenate([act[g * GW // LANES + k] for k in range(GW // LANES)], axis=1)
            xs_b = xs.astype(BF16)
            Bg = act[nx + g].astype(BF16)
            Cg = act[nx + G + g].astype(BF16)
            CB = lax.dot_general(Cg, Bg, (((1,), (1,)), ((), ())), preferred_element_type=F32)
            Hs = state[g]
            y_off = jnp.dot(Cg, Hs.astype(BF16), preferred_element_type=F32)
            Ms, xbd = [], []
            for r in range(R):
                h = g * R + r
                seg = colT[:, h:h + 1] - csd[h:h + 1, :]
                Ms.append((CB * jnp.exp2(jnp.where(causal, seg, -jnp.inf))).astype(BF16))
                xbd.append(jnp.where(head_of_lane == r, xs_b, jnp.zeros_like(xs_b)))
            y_diag = jnp.dot(jnp.concatenate(Ms, axis=1), jnp.concatenate(xbd, axis=0),
                             preferred_element_type=F32)
            expd = jnp.dot(colT3, sel[:, 2 * g * GW:2 * (g + 1) * GW], preferred_element_type=F32)
            wdec_e = expd[:, :GW]
            ecs_e = expd[:, GW:]
            xt_b = (xs * wdec_e).astype(BF16)
            st_new = lax.dot_general(Bg, xt_b, (((0,), (0,)), ((), ())), preferred_element_type=F32)
            state[g] = Hs * ecs_e[L - 1:L, :] + st_new
            y_ref[rows, sl] = (y_diag + y_off * ecs_e + dexp_ref[:, sl] * xs).astype(BF16)
        return carry

    lax.fori_loop(0, xs_ref.shape[0] // L, chunk, 0)


def _mixer(proj, dt_raw, conv_w, conv_b, dt_bias, a_log, d_exp, side, batch, seq, heads, tm):
    T = proj.shape[0]
    nt = seq // tm
    G, N = SSM_GROUPS, SSM_STATE
    d_ssm = heads * SSM_HEADDIM
    d_xbc = conv_w.shape[1]
    assert d_xbc - d_ssm == 2 * G * N == d_ssm and tm % CHUNK == 0 and seq % tm == 0
    blk = lambda k: pl.BlockSpec((tm, d_ssm), lambda b, t: (b * nt + t, k))
    const = lambda b, t: (0, 0)
    side_specs = []
    for s in side:
        rb = _slab_rows(s.shape[0], batch * nt)
        last = s.shape[0] // rb - 1
        side_specs.append(pl.BlockSpec((rb, s.shape[1]),
                                       lambda b, t, last=last: (jnp.minimum(b * nt + t, last), 0)))
    outs = pl.pallas_call(
        functools.partial(_mixer_kernel, heads=heads, nside=len(side)),
        grid=(batch, nt),
        in_specs=[
            blk(1), blk(2),
            pl.BlockSpec((tm, LANES), lambda b, t: (b * nt + t, 0)),
            pl.BlockSpec((SSM_CONV, d_xbc), const),
            pl.BlockSpec((1, d_xbc), const),
            pl.BlockSpec((1, LANES), const),
            pl.BlockSpec((1, LANES), const),
            pl.BlockSpec((1, d_ssm), const),
        ] + side_specs,
        out_specs=[blk(0)] + side_specs,
        out_shape=[jax.ShapeDtypeStruct((T, d_ssm), BF16)] + [jax.ShapeDtypeStruct(s.shape, BF16) for s in side],
        scratch_shapes=[pltpu.VMEM((d_xbc // LANES, SUBLANES + CHUNK, LANES), F32),
                        pltpu.VMEM((d_xbc // LANES, CHUNK, LANES), F32),
                        pltpu.VMEM((G, N, d_ssm // G), F32),
                        pltpu.VMEM((SEL_TERMS * LANES, 2 * d_ssm), BF16)],
        compiler_params=_cparams(("arbitrary", "arbitrary")),
        name="mixer",
    )(proj, proj, dt_raw, conv_w, conv_b, dt_bias, a_log, d_exp, *side)
    return outs[0], outs[1:]


def _causal_conv_rows(u, prev, w, width):
    rows, C = u.shape
    nb = rows // SUBLANES
    full = jnp.concatenate([prev, u], axis=0).reshape(nb + 1, SUBLANES, C)
    sub = lax.broadcasted_iota(jnp.int32, (nb, SUBLANES, C), 1)
    tap = lambda k: jnp.broadcast_to(w[k:k + 1, :], (SUBLANES, C)).reshape(1, SUBLANES, C)
    y = full[1:] * tap(width - 1)
    for k in range(width - 1):
        s = width - 1 - k
        r = pltpu.roll(full, s, 1)
        y = y + jnp.where(sub >= s, r[1:], r[:-1]) * tap(k)
    return y.reshape(rows, C)


def _out_proj_kernel(y_ref, z_ref, ng_ref, gb_ref, gc_ref, u_ref, scw_ref, w_ref, x_ref, h_ref, carry, *, tn,
                     tiles_per_seq):
    tm, Ka = y_ref.shape
    cols = [slice(s * tn, (s + 1) * tn) for s in range(h_ref.shape[1] // tn)]
    p = gc_ref[...].astype(F32) * u_ref[...].astype(F32)
    prev = jnp.where((pl.program_id(0) % tiles_per_seq) == 0, 0.0, carry[...])
    carry[...] = p[tm - SUBLANES:, :]
    yb = (gb_ref[...].astype(F32) * _causal_conv_rows(p, prev, scw_ref[...], SHORT_CONV)).astype(BF16)
    for sl in cols:
        h_ref[:, sl] = x_ref[:, sl] + jnp.dot(yb, w_ref[Ka:, sl], preferred_element_type=F32)
    yg = y_ref[...].astype(F32) * _silu(z_ref[...].astype(F32))
    ya = (yg * _rms_scale(yg) * ng_ref[...]).astype(BF16)
    for sl in cols:
        h_ref[:, sl] += jnp.dot(ya, w_ref[:Ka, sl], preferred_element_type=F32)


def _out_proj(y, proj, norm_g, col0, sc_conv_w, w_out, x, seq, tm, tn):
    T, D = x.shape
    Ka = y.shape[1]
    Kb = sc_conv_w.shape[1]
    assert seq % tm == 0
    blk = lambda k: pl.BlockSpec((tm, Kb), lambda i: (i, col0 + k))
    return pl.pallas_call(
        functools.partial(_out_proj_kernel, tn=tn, tiles_per_seq=seq // tm),
        grid=(T // tm,),
        in_specs=[
            pl.BlockSpec((tm, Ka), lambda i: (i, 0)),
            pl.BlockSpec((tm, Ka), lambda i: (i, 0)),
            pl.BlockSpec((1, Ka), lambda i: (0, 0)),
            blk(0), blk(1), blk(2),
            pl.BlockSpec((SHORT_CONV, Kb), lambda i: (0, 0)),
            pl.BlockSpec((Ka + Kb, D), lambda i: (0, 0), pipeline_mode=pl.Buffered(1)),
            pl.BlockSpec((tm, D), lambda i: (i, 0)),
        ],
        out_specs=pl.BlockSpec((tm, D), lambda i: (i, 0)),
        out_shape=jax.ShapeDtypeStruct((T, D), F32),
        scratch_shapes=[pltpu.VMEM((SUBLANES, Kb), F32)],
        compiler_params=_cparams(("arbitrary",)),
        name="out_proj",
    )(y, proj, norm_g, proj, proj, proj, sc_conv_w, w_out, x)


def _ffn_kernel(h_ref, g_ref, wg_ref, wu_ref, wd_ref, gf_ref, o_ref, n_scr, *, final_norm, nsplit):
    f = pl.program_id(1)

    @pl.when(f == 0)
    def _():
        h = h_ref[...]
        n_scr[...] = (h * _rms_scale(h) * g_ref[...]).astype(BF16)
        o_ref[...] = h

    n = n_scr[...]
    gate = jnp.dot(n, wg_ref[...], preferred_element_type=F32)
    up = jnp.dot(n, wu_ref[...], preferred_element_type=F32)
    a = (_silu(gate) * up).astype(BF16)
    wn = o_ref.shape[1] // nsplit
    for s in range(nsplit):
        sl = slice(s * wn, (s + 1) * wn)
        o_ref[:, sl] += jnp.dot(a, wd_ref[:, sl], preferred_element_type=F32)

    if final_norm:
        @pl.when(f == pl.num_programs(1) - 1)
        def _():
            h2 = o_ref[...]
            o_ref[...] = h2 * _rms_scale(h2) * gf_ref[...]


def _ffn(h1, g, w_gate, w_up, w_down, g_final, final_norm, tm, tf):
    T, D = h1.shape
    F = w_gate.shape[1]
    return pl.pallas_call(
        functools.partial(_ffn_kernel, final_norm=final_norm, nsplit=4),
        grid=(T // tm, F // tf),
        in_specs=[
            pl.BlockSpec((tm, D), lambda i, f: (i, 0)),
            pl.BlockSpec((1, D), lambda i, f: (0, 0)),
            pl.BlockSpec((D, tf), lambda i, f: (0, f)),
            pl.BlockSpec((D, tf), lambda i, f: (0, f)),
            pl.BlockSpec((tf, D), lambda i, f: (f, 0)),
            pl.BlockSpec((1, D), lambda i, f: (0, 0)),
        ],
        out_specs=pl.BlockSpec((tm, D), lambda i, f: (i, 0)),
        out_shape=jax.ShapeDtypeStruct((T, D), F32),
        scratch_shapes=[pltpu.VMEM((tm, D), BF16)],
        compiler_params=_cparams(("parallel", "arbitrary")),
        name="ffn",
    )(h1, g, w_gate, w_up, w_down, g_final)


def _pad_lanes(v):
    return jnp.pad(v.reshape(1, -1), ((0, 0), (0, LANES - v.shape[-1])))


def kernel(x, norm_mix_g, w_in, ssm_conv_w, ssm_conv_b, ssm_dt_bias, ssm_A_log, ssm_D, ssm_norm_g,
           sc_conv_w, w_out, norm_ffn_g, w_gate, w_up, w_down, norm_final_g):
    batch, seq, d_model = x.shape
    depth = w_in.shape[0]
    d_ssm = ssm_norm_g.shape[1]
    d_xbc = ssm_conv_w.shape[2]
    heads = ssm_dt_bias.shape[1]
    d_conv = sc_conv_w.shape[2]
    assert d_ssm == d_conv == d_model and heads * SSM_HEADDIM == d_ssm and 3 * heads <= LANES
    assert seq % 1024 == 0
    off_dt = d_ssm + d_xbc
    off_cb = off_dt + heads
    assert w_in.shape[2] - off_cb == off_dt

    h = x.reshape(batch * seq, d_model)
    for l in range(depth):
        w_proj, w_dt = _prep_w_in(w_in[l].T, off_dt, heads, tn=512)
        proj, dt_raw = _in_proj(h, norm_mix_g[l].reshape(1, -1), w_proj, w_dt, tm=1024, tn=2048)
        y, (wo_b, wg_b, wu_b, wd_b) = _mixer(
            proj, dt_raw, ssm_conv_w[l], ssm_conv_b[l].reshape(1, -1), _pad_lanes(ssm_dt_bias[l]),
            _pad_lanes(ssm_A_log[l]), jnp.repeat(ssm_D[l], SSM_HEADDIM).reshape(1, -1),
            (w_out[l], w_gate[l], w_up[l], w_down[l]), batch, seq, heads, tm=256)
        h1 = _out_proj(y, proj, ssm_norm_g[l].reshape(1, -1), (d_ssm + d_xbc) // d_conv, sc_conv_w[l], wo_b, h,
                       seq, tm=512, tn=512)
        last = l == depth - 1
        h = _ffn(h1, norm_ffn_g[l].reshape(1, -1), wg_b, wu_b, wd_b, norm_final_g.reshape(1, -1),
                 final_norm=last, tm=1024, tf=512)
    return h.reshape(batch, seq, d_model)
```

```python
import functools

import jax
import jax.numpy as jnp
from jax import lax
from jax.experimental import pallas as pl
from jax.experimental.pallas import tpu as pltpu

F32 = jnp.float32
BF16 = jnp.bfloat16

EPS = 1e-5
LOG2E = 1.4426950408889634
SSM_HEADDIM = 64
SSM_GROUPS = 8
SSM_STATE = 128
SSM_CONV = 4
SHORT_CONV = 3
CHUNK = 128
SUBLANES = 8
LANES = 128
BF16_ROWS = 16
PHASES = 4
SEL_TERMS = 2
VMEM_LIMIT = 60 * 1024 * 1024


def _cparams(sem):
    return pltpu.CompilerParams(dimension_semantics=sem, vmem_limit_bytes=VMEM_LIMIT)


def _rms_scale(x):
    return lax.rsqrt(jnp.mean(x * x, axis=-1, keepdims=True) + EPS)


def _silu_from_half(h):
    return h + h * jnp.tanh(h)


def _silu(x):
    return _silu_from_half(0.5 * x)


def _split3(v):
    hi = v.astype(BF16)
    r1 = v - hi.astype(F32)
    mid = r1.astype(BF16)
    lo = (r1 - mid.astype(F32)).astype(BF16)
    return hi, mid, lo


def _prep_kernel(blk_ref, nxt_ref, w_ref, wdt_ref, *, nt, shift):
    k = pl.program_id(0)
    tn = blk_ref.shape[0]

    @pl.when(k < nt)
    def _():
        w_ref[...] = blk_ref[...].astype(BF16)

    @pl.when(k >= nt)
    def _():
        w_ref[:tn - shift, :] = blk_ref[shift:, :].astype(BF16)
        w_ref[tn - shift:, :] = nxt_ref[...].astype(BF16)

    @pl.when(k == nt)
    def _():
        wdt_ref[:shift, :] = blk_ref[:shift, :].astype(BF16)
        wdt_ref[shift:, :] = jnp.zeros((LANES - shift, wdt_ref.shape[1]), BF16)


def _prep_w_in(wt, off_b, shift, tn):
    D = wt.shape[1]
    nt = off_b // tn
    assert off_b % tn == 0 and tn % shift == 0 and shift % BF16_ROWS == 0 and wt.shape[0] == 2 * off_b + shift
    return pl.pallas_call(
        functools.partial(_prep_kernel, nt=nt, shift=shift),
        grid=(2 * nt,),
        in_specs=[
            pl.BlockSpec((tn, D), lambda k: (k, 0)),
            pl.BlockSpec((shift, D), lambda k: ((k + 1) * (tn // shift), 0)),
        ],
        out_specs=[
            pl.BlockSpec((tn, D), lambda k: (k, 0)),
            pl.BlockSpec((LANES, D), lambda k: (0, 0)),
        ],
        out_shape=[jax.ShapeDtypeStruct((2 * off_b, D), BF16), jax.ShapeDtypeStruct((LANES, D), BF16)],
        compiler_params=_cparams(("arbitrary",)),
        name="prep_w_in",
    )(wt, wt)


def _in_proj_kernel(x_ref, g_ref, w_ref, wdt_ref, out_ref, dt_ref, n_scr, *, ncol):
    def mm(lhs, wt):
        return lax.dot_general(lhs, wt, (((1,), (1,)), ((), ())), preferred_element_type=F32)

    @pl.when(pl.program_id(1) == 0)
    def _():
        x = x_ref[...]
        n = (x * _rms_scale(x) * g_ref[...]).astype(BF16)
        n_scr[...] = n
        dt_ref[...] = mm(n, wdt_ref[...])

    for s in range(out_ref.shape[1] // ncol):
        cols = slice(s * ncol, (s + 1) * ncol)
        out_ref[:, cols] = mm(n_scr[...], w_ref[cols, :]).astype(BF16)


def _in_proj(x, g, w, w_dt, tm, tn):
    T, D = x.shape
    N = w.shape[0]
    return pl.pallas_call(
        functools.partial(_in_proj_kernel, ncol=min(tn, 1024)),
        grid=(T // tm, N // tn),
        in_specs=[
            pl.BlockSpec((tm, D), lambda i, j: (i, 0)),
            pl.BlockSpec((1, D), lambda i, j: (0, 0)),
            pl.BlockSpec((tn, D), lambda i, j: (j, 0)),
            pl.BlockSpec((LANES, D), lambda i, j: (0, 0)),
        ],
        out_specs=[
            pl.BlockSpec((tm, tn), lambda i, j: (i, j)),
            pl.BlockSpec((tm, LANES), lambda i, j: (i, 0)),
        ],
        out_shape=[jax.ShapeDtypeStruct((T, N), BF16), jax.ShapeDtypeStruct((T, LANES), F32)],
        scratch_shapes=[pltpu.VMEM((tm, D), BF16)],
        compiler_params=_cparams(("parallel", "arbitrary")),
        name="in_proj",
    )(x, g, w, w_dt)


def _slab_rows(rows, steps):
    per = -(-rows // steps)
    per = -(-per // BF16_ROWS) * BF16_ROWS
    while rows % per:
        per += BF16_ROWS
    return per


def _conv_phases(buf, c, w, width, rows):
    n = rows // PHASES
    shifted = {m: buf[c, pl.ds(SUBLANES + m, n, stride=PHASES), :] for m in range(1 - width, PHASES)}
    out = []
    for p in range(PHASES):
        acc = None
        for k in range(width):
            term = w[k:k + 1, :] * shifted[p - (width - 1) + k]
            acc = term if acc is None else acc + term
        out.append(acc)
    return out


def _store_phases(dst, c, phases, rows):
    for p, v in enumerate(phases):
        dst[c, pl.ds(p, rows // PHASES, stride=PHASES), :] = v


def _mixer_kernel(*refs, heads, nside):
    xs_ref, bc_ref, dt_ref, cw_ref, cb_ref, dtb_ref, alog_ref, dexp_ref = refs[:8]
    side_in = refs[8:8 + nside]
    y_ref = refs[8 + nside]
    side_out = refs[9 + nside:9 + 2 * nside]
    bufs, acts, state, sel = refs[9 + 2 * nside:]
    L = CHUNK
    for src, dst in zip(side_in, side_out):
        dst[...] = src[...].astype(BF16)
    G, N, P = SSM_GROUPS, SSM_STATE, SSM_HEADDIM
    R = heads // G
    GW = R * P
    nx = heads * P // LANES
    lane = lambda c: slice(c * LANES, (c + 1) * LANES)

    @pl.when(pl.program_id(1) == 0)
    def _():
        bufs[0, :, 0:SUBLANES, :] = jnp.zeros((bufs.shape[1], SUBLANES, LANES), F32)
        state[...] = jnp.zeros(state.shape, F32)
        src_lane = lax.broadcasted_iota(jnp.int32, (LANES, 2 * heads * P), 0)
        out_col = lax.broadcasted_iota(jnp.int32, (LANES, 2 * heads * P), 1)
        grp = out_col // (2 * GW)
        kind = (out_col // GW) % 2
        head = grp * R + (out_col % GW) // P
        one = (src_lane == heads * (1 + kind) + head).astype(BF16)
        for k in range(SEL_TERMS):
            sel[k * LANES:(k + 1) * LANES, :] = one

    def chunk(ci, parity):
        buf, act, buf_next = bufs.at[parity], acts.at[parity], bufs.at[1 - parity]
        rows = pl.ds(pl.multiple_of(ci * L, L), L)

        dt_in = (dt_ref[rows, :] + dtb_ref[...]).T[:heads]
        dt = jnp.maximum(dt_in, 0.0) + jnp.log1p(jnp.exp(-jnp.abs(dt_in)))
        a2 = jnp.broadcast_to(-LOG2E * jnp.exp(alog_ref[...]), (L, LANES)).T[:heads]
        dA2 = dt * a2
        row = lax.broadcasted_iota(jnp.int32, (L, L), 0)
        col = lax.broadcasted_iota(jnp.int32, (L, L), 1)
        causal = row >= col
        triu = (row <= col).astype(BF16)
        cs2 = jnp.dot(jnp.concatenate(_split3(dA2), axis=1), jnp.concatenate([triu, triu, triu], axis=0),
                      preferred_element_type=F32)
        cs2_last = cs2[:, L - 1:L]
        ecs = jnp.exp2(cs2)
        wdec = dt * jnp.exp2(cs2_last - cs2)
        csd = cs2 - jnp.log2(dt)
        colT = jnp.concatenate([cs2, wdec, ecs, jnp.zeros((LANES - 3 * heads, L), F32)], axis=0).T
        colT3 = jnp.concatenate(_split3(colT)[:SEL_TERMS], axis=1)

        for c in range(buf.shape[0]):
            src = xs_ref[rows, lane(c)] if c < nx else bc_ref[rows, lane(c - nx)]
            buf[c, SUBLANES:, :] = src.astype(F32)
        for c in range(buf.shape[0]):
            halves = _conv_phases(buf, c, 0.5 * cw_ref[:, lane(c)], SSM_CONV, L)
            b = 0.5 * cb_ref[:, lane(c)]
            _store_phases(act, c, [_silu_from_half(v + b) for v in halves], L)
            buf_next[c, 0:SUBLANES, :] = buf[c, L:L + SUBLANES, :]

        head_of_lane = lax.broadcasted_iota(jnp.int32, (L, GW), 1) // P
        for g in range(G):
            sl = slice(g * GW, (g + 1) * GW)
            xs = jnp.concatenate([act[g * GW // LANES + k] for k in range(GW // LANES)], axis=1)
            xs_b = xs.astype(BF16)
            Bg = act[nx + g].astype(BF16)
            Cg = act[nx + G + g].astype(BF16)
            CB = lax.dot_general(Cg, Bg, (((1,), (1,)), ((), ())), preferred_element_type=F32)
            Hs = state[g]
            y_off = jnp.dot(Cg, Hs.astype(BF16), preferred_element_type=F32)
            Ms, xbd = [], []
            for r in range(R):
                h = g * R + r
                seg = colT[:, h:h + 1] - csd[h:h + 1, :]
                Ms.append((CB * jnp.exp2(jnp.where(causal, seg, -jnp.inf))).astype(BF16))
                xbd.append(jnp.where(head_of_lane == r, xs_b, jnp.zeros_like(xs_b)))
            y_diag = jnp.dot(jnp.concatenate(Ms, axis=1), jnp.concatenate(xbd, axis=0),
                             preferred_element_type=F32)
            expd = jnp.dot(colT3, sel[:, 2 * g * GW:2 * (g + 1) * GW], preferred_element_type=F32)
            wdec_e = expd[:, :GW]
            ecs_e = expd[:, GW:]
            xt_b = (xs * wdec_e).astype(BF16)
            st_new = lax.dot_general(Bg, xt_b, (((0,), (0,)), ((), ())), preferred_element_type=F32)
            state[g] = Hs * ecs_e[L - 1:L, :] + st_new
            y_ref[rows, sl] = (y_diag + y_off * ecs_e + dexp_ref[:, sl] * xs).astype(BF16)

    def chunk_pair(pi, carry):
        chunk(2 * pi, 0)
        chunk(2 * pi + 1, 1)
        return carry

    lax.fori_loop(0, xs_ref.shape[0] // (2 * L), chunk_pair, 0)


def _mixer(proj, dt_raw, conv_w, conv_b, dt_bias, a_log, d_exp, side, batch, seq, heads, tm):
    T = proj.shape[0]
    nt = seq // tm
    G, N = SSM_GROUPS, SSM_STATE
    d_ssm = heads * SSM_HEADDIM
    d_xbc = conv_w.shape[1]
    assert d_xbc - d_ssm == 2 * G * N == d_ssm and tm % (2 * CHUNK) == 0 and seq % tm == 0
    blk = lambda k: pl.BlockSpec((tm, d_ssm), lambda b, t: (b * nt + t, k))
    const = lambda b, t: (0, 0)
    side_specs = []
    for s in side:
        rb = _slab_rows(s.shape[0], batch * nt)
        last = s.shape[0] // rb - 1
        side_specs.append(pl.BlockSpec((rb, s.shape[1]),
                                       lambda b, t, last=last: (jnp.minimum(b * nt + t, last), 0)))
    outs = pl.pallas_call(
        functools.partial(_mixer_kernel, heads=heads, nside=len(side)),
        grid=(batch, nt),
        in_specs=[
            blk(1), blk(2),
            pl.BlockSpec((tm, LANES), lambda b, t: (b * nt + t, 0)),
            pl.BlockSpec((SSM_CONV, d_xbc), const),
            pl.BlockSpec((1, d_xbc), const),
            pl.BlockSpec((1, LANES), const),
            pl.BlockSpec((1, LANES), const),
            pl.BlockSpec((1, d_ssm), const),
        ] + side_specs,
        out_specs=[blk(0)] + side_specs,
        out_shape=[jax.ShapeDtypeStruct((T, d_ssm), BF16)] + [jax.ShapeDtypeStruct(s.shape, BF16) for s in side],
        scratch_shapes=[pltpu.VMEM((2, d_xbc // LANES, SUBLANES + CHUNK, LANES), F32),
                        pltpu.VMEM((2, d_xbc // LANES, CHUNK, LANES), F32),
                        pltpu.VMEM((G, N, d_ssm // G), F32),
                        pltpu.VMEM((SEL_TERMS * LANES, 2 * d_ssm), BF16)],
        compiler_params=_cparams(("arbitrary", "arbitrary")),
        name="mixer",
    )(proj, proj, dt_raw, conv_w, conv_b, dt_bias, a_log, d_exp, *side)
    return outs[0], outs[1:]


def _causal_conv_rows(u, prev, w, width):
    rows, C = u.shape
    nb = rows // SUBLANES
    full = jnp.concatenate([prev, u], axis=0).reshape(nb + 1, SUBLANES, C)
    sub = lax.broadcasted_iota(jnp.int32, (nb, SUBLANES, C), 1)
    tap = lambda k: jnp.broadcast_to(w[k:k + 1, :], (SUBLANES, C)).reshape(1, SUBLANES, C)
    y = full[1:] * tap(width - 1)
    for k in range(width - 1):
        s = width - 1 - k
        r = pltpu.roll(full, s, 1)
        y = y + jnp.where(sub >= s, r[1:], r[:-1]) * tap(k)
    return y.reshape(rows, C)


def _out_proj_kernel(y_ref, z_ref, ng_ref, gb_ref, gc_ref, u_ref, scw_ref, w_ref, x_ref, h_ref, carry, *, tn,
                     tiles_per_seq):
    tm, Ka = y_ref.shape
    cols = [slice(s * tn, (s + 1) * tn) for s in range(h_ref.shape[1] // tn)]
    p = gc_ref[...].astype(F32) * u_ref[...].astype(F32)
    prev = jnp.where((pl.program_id(0) % tiles_per_seq) == 0, 0.0, carry[...])
    carry[...] = p[tm - SUBLANES:, :]
    yb = (gb_ref[...].astype(F32) * _causal_conv_rows(p, prev, scw_ref[...], SHORT_CONV)).astype(BF16)
    for sl in cols:
        h_ref[:, sl] = x_ref[:, sl] + jnp.dot(yb, w_ref[Ka:, sl], preferred_element_type=F32)
    yg = y_ref[...].astype(F32) * _silu(z_ref[...].astype(F32))
    ya = (yg * _rms_scale(yg) * ng_ref[...]).astype(BF16)
    for sl in cols:
        h_ref[:, sl] += jnp.dot(ya, w_ref[:Ka, sl], preferred_element_type=F32)


def _out_proj(y, proj, norm_g, col0, sc_conv_w, w_out, x, seq, tm, tn):
    T, D = x.shape
    Ka = y.shape[1]
    Kb = sc_conv_w.shape[1]
    assert seq % tm == 0
    blk = lambda k: pl.BlockSpec((tm, Kb), lambda i: (i, col0 + k))
    return pl.pallas_call(
        functools.partial(_out_proj_kernel, tn=tn, tiles_per_seq=seq // tm),
        grid=(T // tm,),
        in_specs=[
            pl.BlockSpec((tm, Ka), lambda i: (i, 0)),
            pl.BlockSpec((tm, Ka), lambda i: (i, 0)),
            pl.BlockSpec((1, Ka), lambda i: (0, 0)),
            blk(0), blk(1), blk(2),
            pl.BlockSpec((SHORT_CONV, Kb), lambda i: (0, 0)),
            pl.BlockSpec((Ka + Kb, D), lambda i: (0, 0), pipeline_mode=pl.Buffered(1)),
            pl.BlockSpec((tm, D), lambda i: (i, 0)),
        ],
        out_specs=pl.BlockSpec((tm, D), lambda i: (i, 0)),
        out_shape=jax.ShapeDtypeStruct((T, D), F32),
        scratch_shapes=[pltpu.VMEM((SUBLANES, Kb), F32)],
        compiler_params=_cparams(("arbitrary",)),
        name="out_proj",
    )(y, proj, norm_g, proj, proj, proj, sc_conv_w, w_out, x)


def _ffn_kernel(h_ref, g_ref, wg_ref, wu_ref, wd_ref, gf_ref, o_ref, n_scr, *, final_norm, nsplit):
    f = pl.program_id(1)

    @pl.when(f == 0)
    def _():
        h = h_ref[...]
        n_scr[...] = (h * _rms_scale(h) * g_ref[...]).astype(BF16)
        o_ref[...] = h

    n = n_scr[...]
    gate = jnp.dot(n, wg_ref[...], preferred_element_type=F32)
    up = jnp.dot(n, wu_ref[...], preferred_element_type=F32)
    a = (_silu(gate) * up).astype(BF16)
    wn = o_ref.shape[1] // nsplit
    for s in range(nsplit):
        sl = slice(s * wn, (s + 1) * wn)
        o_ref[:, sl] += jnp.dot(a, wd_ref[:, sl], preferred_element_type=F32)

    if final_norm:
        @pl.when(f == pl.num_programs(1) - 1)
        def _():
            h2 = o_ref[...]
            o_ref[...] = h2 * _rms_scale(h2) * gf_ref[...]


def _ffn(h1, g, w_gate, w_up, w_down, g_final, final_norm, tm, tf):
    T, D = h1.shape
    F = w_gate.shape[1]
    return pl.pallas_call(
        functools.partial(_ffn_kernel, final_norm=final_norm, nsplit=4),
        grid=(T // tm, F // tf),
        in_specs=[
            pl.BlockSpec((tm, D), lambda i, f: (i, 0)),
            pl.BlockSpec((1, D), lambda i, f: (0, 0)),
            pl.BlockSpec((D, tf), lambda i, f: (0, f)),
            pl.BlockSpec((D, tf), lambda i, f: (0, f)),
            pl.BlockSpec((tf, D), lambda i, f: (f, 0)),
            pl.BlockSpec((1, D), lambda i, f: (0, 0)),
        ],
        out_specs=pl.BlockSpec((tm, D), lambda i, f: (i, 0)),
        out_shape=jax.ShapeDtypeStruct((T, D), F32),
        scratch_shapes=[pltpu.VMEM((tm, D), BF16)],
        compiler_params=_cparams(("parallel", "arbitrary")),
        name="ffn",
    )(h1, g, w_gate, w_up, w_down, g_final)


def _pad_lanes(v):
    return jnp.pad(v.reshape(1, -1), ((0, 0), (0, LANES - v.shape[-1])))


def kernel(x, norm_mix_g, w_in, ssm_conv_w, ssm_conv_b, ssm_dt_bias, ssm_A_log, ssm_D, ssm_norm_g,
           sc_conv_w, w_out, norm_ffn_g, w_gate, w_up, w_down, norm_final_g):
    batch, seq, d_model = x.shape
    depth = w_in.shape[0]
    d_ssm = ssm_norm_g.shape[1]
    d_xbc = ssm_conv_w.shape[2]
    heads = ssm_dt_bias.shape[1]
    d_conv = sc_conv_w.shape[2]
    assert d_ssm == d_conv == d_model and heads * SSM_HEADDIM == d_ssm and 3 * heads <= LANES
    assert seq % 1024 == 0
    off_dt = d_ssm + d_xbc
    off_cb = off_dt + heads
    assert w_in.shape[2] - off_cb == off_dt

    h = x.reshape(batch * seq, d_model)
    for l in range(depth):
        w_proj, w_dt = _prep_w_in(w_in[l].T, off_dt, heads, tn=512)
        proj, dt_raw = _in_proj(h, norm_mix_g[l].reshape(1, -1), w_proj, w_dt, tm=1024, tn=2048)
        y, (wo_b, wg_b, wu_b, wd_b) = _mixer(
            proj, dt_raw, ssm_conv_w[l], ssm_conv_b[l].reshape(1, -1), _pad_lanes(ssm_dt_bias[l]),
            _pad_lanes(ssm_A_log[l]), jnp.repeat(ssm_D[l], SSM_HEADDIM).reshape(1, -1),
            (w_out[l], w_gate[l], w_up[l], w_down[l]), batch, seq, heads, tm=256)
        h1 = _out_proj(y, proj, ssm_norm_g[l].reshape(1, -1), (d_ssm + d_xbc) // d_conv, sc_conv_w[l], wo_b, h,
                       seq, tm=512, tn=512)
        last = l == depth - 1
        h = _ffn(h1, norm_ffn_g[l].reshape(1, -1), wg_b, wu_b, wd_b, norm_final_g.reshape(1, -1),
                 final_norm=last, tm=1024, tf=512)
    return h.reshape(batch, seq, d_model)
```

```python
import functools

import jax
import jax.numpy as jnp
from jax import lax
from jax.experimental import pallas as pl
from jax.experimental.pallas import tpu as pltpu

F32 = jnp.float32
BF16 = jnp.bfloat16

EPS = 1e-5
LOG2E = 1.4426950408889634
SSM_HEADDIM = 64
SSM_GROUPS = 8
SSM_STATE = 128
SSM_CONV = 4
SHORT_CONV = 3
CHUNK = 128
SUBLANES = 8
LANES = 128
BF16_ROWS = 16
PHASES = 4
W_SLOTS = 3
SEL_TERMS = 2
VMEM_LIMIT = 60 * 1024 * 1024


def _cparams(sem):
    return pltpu.CompilerParams(dimension_semantics=sem, vmem_limit_bytes=VMEM_LIMIT)


def _rms_scale(x):
    return lax.rsqrt(jnp.mean(x * x, axis=-1, keepdims=True) + EPS)


def _silu_from_half(h):
    return h + h * jnp.tanh(h)


def _silu(x):
    return _silu_from_half(0.5 * x)


def _split3(v):
    hi = v.astype(BF16)
    r1 = v - hi.astype(F32)
    mid = r1.astype(BF16)
    lo = (r1 - mid.astype(F32)).astype(BF16)
    return hi, mid, lo


def _prep_kernel(blk_ref, nxt_ref, w_ref, wdt_ref, *, nt, shift):
    k = pl.program_id(0)
    tn = blk_ref.shape[0]

    @pl.when(k < nt)
    def _():
        w_ref[...] = blk_ref[...].astype(BF16)

    @pl.when(k >= nt)
    def _():
        w_ref[:tn - shift, :] = blk_ref[shift:, :].astype(BF16)
        w_ref[tn - shift:, :] = nxt_ref[...].astype(BF16)

    @pl.when(k == nt)
    def _():
        wdt_ref[:shift, :] = blk_ref[:shift, :].astype(BF16)
        wdt_ref[shift:, :] = jnp.zeros((LANES - shift, wdt_ref.shape[1]), BF16)


def _prep_w_in(wt, off_b, shift, tn):
    D = wt.shape[1]
    nt = off_b // tn
    assert off_b % tn == 0 and tn % shift == 0 and shift % BF16_ROWS == 0 and wt.shape[0] == 2 * off_b + shift
    return pl.pallas_call(
        functools.partial(_prep_kernel, nt=nt, shift=shift),
        grid=(2 * nt,),
        in_specs=[
            pl.BlockSpec((tn, D), lambda k: (k, 0)),
            pl.BlockSpec((shift, D), lambda k: ((k + 1) * (tn // shift), 0)),
        ],
        out_specs=[
            pl.BlockSpec((tn, D), lambda k: (k, 0)),
            pl.BlockSpec((LANES, D), lambda k: (0, 0)),
        ],
        out_shape=[jax.ShapeDtypeStruct((2 * off_b, D), BF16), jax.ShapeDtypeStruct((LANES, D), BF16)],
        compiler_params=_cparams(("arbitrary",)),
        name="prep_w_in",
    )(wt, wt)


def _in_proj_kernel(x_ref, g_ref, w_hbm, wdt_ref, out_ref, dt_ref, n_scr, wbuf, sem, *, ncol):
    nj = pl.num_programs(1)
    step = pl.program_id(0) * nj + pl.program_id(1)
    total = pl.num_programs(0) * nj
    tn = wbuf.shape[1]

    def tile_copy(t):
        return pltpu.make_async_copy(w_hbm.at[pl.ds(pl.multiple_of((t % nj) * tn, tn), tn), :],
                                     wbuf.at[t % W_SLOTS], sem.at[t % W_SLOTS])

    @pl.when(step == 0)
    def _():
        for t in range(W_SLOTS - 1):
            tile_copy(step + t).start()

    @pl.when(step + W_SLOTS - 1 < total)
    def _():
        tile_copy(step + W_SLOTS - 1).start()

    def mm(lhs, wt):
        return lax.dot_general(lhs, wt, (((1,), (1,)), ((), ())), preferred_element_type=F32)

    @pl.when(pl.program_id(1) == 0)
    def _():
        x = x_ref[...]
        n = (x * _rms_scale(x) * g_ref[...]).astype(BF16)
        n_scr[...] = n
        dt_ref[...] = mm(n, wdt_ref[...])

    tile_copy(step).wait()
    w_ref = wbuf.at[step % W_SLOTS]
    for s in range(out_ref.shape[1] // ncol):
        cols = slice(s * ncol, (s + 1) * ncol)
        out_ref[:, cols] = mm(n_scr[...], w_ref[cols, :]).astype(BF16)


def _in_proj(x, g, w, w_dt, tm, tn):
    T, D = x.shape
    N = w.shape[0]
    return pl.pallas_call(
        functools.partial(_in_proj_kernel, ncol=min(tn, 1024)),
        grid=(T // tm, N // tn),
        in_specs=[
            pl.BlockSpec((tm, D), lambda i, j: (i, 0)),
            pl.BlockSpec((1, D), lambda i, j: (0, 0)),
            pl.BlockSpec(memory_space=pl.ANY),
            pl.BlockSpec((LANES, D), lambda i, j: (0, 0)),
        ],
        out_specs=[
            pl.BlockSpec((tm, tn), lambda i, j: (i, j)),
            pl.BlockSpec((tm, LANES), lambda i, j: (i, 0)),
        ],
        out_shape=[jax.ShapeDtypeStruct((T, N), BF16), jax.ShapeDtypeStruct((T, LANES), F32)],
        scratch_shapes=[pltpu.VMEM((tm, D), BF16),
                        pltpu.VMEM((W_SLOTS, tn, D), BF16),
                        pltpu.SemaphoreType.DMA((W_SLOTS,))],
        compiler_params=_cparams(("arbitrary", "arbitrary")),
        name="in_proj",
    )(x, g, w, w_dt)


def _slab_rows(rows, steps):
    per = -(-rows // steps)
    per = -(-per // BF16_ROWS) * BF16_ROWS
    while rows % per:
        per += BF16_ROWS
    return per


def _conv_phases(buf, c, w, width, rows):
    n = rows // PHASES
    shifted = {m: buf[c, pl.ds(SUBLANES + m, n, stride=PHASES), :] for m in range(1 - width, PHASES)}
    out = []
    for p in range(PHASES):
        acc = None
        for k in range(width):
            term = w[k:k + 1, :] * shifted[p - (width - 1) + k]
            acc = term if acc is None else acc + term
        out.append(acc)
    return out


def _store_phases(dst, c, phases, rows):
    for p, v in enumerate(phases):
        dst[c, pl.ds(p, rows // PHASES, stride=PHASES), :] = v


def _mixer_kernel(*refs, heads, nside):
    xs_ref, bc_ref, dt_ref, cw_ref, cb_ref, dtb_ref, alog_ref, dexp_ref = refs[:8]
    side_in = refs[8:8 + nside]
    y_ref = refs[8 + nside]
    side_out = refs[9 + nside:9 + 2 * nside]
    bufs, acts, state, sel = refs[9 + 2 * nside:]
    L = CHUNK
    for src, dst in zip(side_in, side_out):
        dst[...] = src[...].astype(BF16)
    G, N, P = SSM_GROUPS, SSM_STATE, SSM_HEADDIM
    R = heads // G
    GW = R * P
    nx = heads * P // LANES
    lane = lambda c: slice(c * LANES, (c + 1) * LANES)

    @pl.when(pl.program_id(1) == 0)
    def _():
        bufs[0, :, 0:SUBLANES, :] = jnp.zeros((bufs.shape[1], SUBLANES, LANES), F32)
        state[...] = jnp.zeros(state.shape, F32)
        src_lane = lax.broadcasted_iota(jnp.int32, (LANES, 2 * heads * P), 0)
        out_col = lax.broadcasted_iota(jnp.int32, (LANES, 2 * heads * P), 1)
        grp = out_col // (2 * GW)
        kind = (out_col // GW) % 2
        head = grp * R + (out_col % GW) // P
        one = (src_lane == heads * (1 + kind) + head).astype(BF16)
        for k in range(SEL_TERMS):
            sel[k * LANES:(k + 1) * LANES, :] = one

    def chunk(ci, parity):
        buf, act, buf_next = bufs.at[parity], acts.at[parity], bufs.at[1 - parity]
        rows = pl.ds(pl.multiple_of(ci * L, L), L)

        dt_in = (dt_ref[rows, :] + dtb_ref[...]).T[:heads]
        dt = jnp.maximum(dt_in, 0.0) + jnp.log1p(jnp.exp(-jnp.abs(dt_in)))
        a2 = jnp.broadcast_to(-LOG2E * jnp.exp(alog_ref[...]), (L, LANES)).T[:heads]
        dA2 = dt * a2
        row = lax.broadcasted_iota(jnp.int32, (L, L), 0)
        col = lax.broadcasted_iota(jnp.int32, (L, L), 1)
        causal = row >= col
        triu = (row <= col).astype(BF16)
        cs2 = jnp.dot(jnp.concatenate(_split3(dA2), axis=1), jnp.concatenate([triu, triu, triu], axis=0),
                      preferred_element_type=F32)
        cs2_last = cs2[:, L - 1:L]
        ecs = jnp.exp2(cs2)
        wdec = dt * jnp.exp2(cs2_last - cs2)
        csd = cs2 - jnp.log2(dt)
        colT = jnp.concatenate([cs2, wdec, ecs, jnp.zeros((LANES - 3 * heads, L), F32)], axis=0).T
        colT3 = jnp.concatenate(_split3(colT)[:SEL_TERMS], axis=1)

        for c in range(buf.shape[0]):
            src = xs_ref[rows, lane(c)] if c < nx else bc_ref[rows, lane(c - nx)]
            buf[c, SUBLANES:, :] = src.astype(F32)
        for c in range(buf.shape[0]):
            halves = _conv_phases(buf, c, 0.5 * cw_ref[:, lane(c)], SSM_CONV, L)
            b = 0.5 * cb_ref[:, lane(c)]
            _store_phases(act, c, [_silu_from_half(v + b) for v in halves], L)
            buf_next[c, 0:SUBLANES, :] = buf[c, L:L + SUBLANES, :]

        head_of_lane = lax.broadcasted_iota(jnp.int32, (L, GW), 1) // P
        for g in range(G):
            sl = slice(g * GW, (g + 1) * GW)
            xs = jnp.concatenate([act[g * GW // LANES + k] for k in range(GW // LANES)], axis=1)
            xs_b = xs.astype(BF16)
            Bg = act[nx + g].astype(BF16)
            Cg = act[nx + G + g].astype(BF16)
            CB = lax.dot_general(Cg, Bg, (((1,), (1,)), ((), ())), preferred_element_type=F32)
            Hs = state[g]
            y_off = jnp.dot(Cg, Hs.astype(BF16), preferred_element_type=F32)
            Ms, xbd = [], []
            for r in range(R):
                h = g * R + r
                seg = colT[:, h:h + 1] - csd[h:h + 1, :]
                Ms.append((CB * jnp.exp2(jnp.where(causal, seg, -jnp.inf))).astype(BF16))
                xbd.append(jnp.where(head_of_lane == r, xs_b, jnp.zeros_like(xs_b)))
            y_diag = jnp.dot(jnp.concatenate(Ms, axis=1), jnp.concatenate(xbd, axis=0),
                             preferred_element_type=F32)
            expd = jnp.dot(colT3, sel[:, 2 * g * GW:2 * (g + 1) * GW], preferred_element_type=F32)
            wdec_e = expd[:, :GW]
            ecs_e = expd[:, GW:]
            xt_b = (xs * wdec_e).astype(BF16)
            st_new = lax.dot_general(Bg, xt_b, (((0,), (0,)), ((), ())), preferred_element_type=F32)
            state[g] = Hs * ecs_e[L - 1:L, :] + st_new
            y_ref[rows, sl] = (y_diag + y_off * ecs_e + dexp_ref[:, sl] * xs).astype(BF16)

    def chunk_pair(pi, carry):
        chunk(2 * pi, 0)
        chunk(2 * pi + 1, 1)
        return carry

    lax.fori_loop(0, xs_ref.shape[0] // (2 * L), chunk_pair, 0)


def _mixer(proj, dt_raw, conv_w, conv_b, dt_bias, a_log, d_exp, side, batch, seq, heads, tm):
    T = proj.shape[0]
    nt = seq // tm
    G, N = SSM_GROUPS, SSM_STATE
    d_ssm = heads * SSM_HEADDIM
    d_xbc = conv_w.shape[1]
    assert d_xbc - d_ssm == 2 * G * N == d_ssm and tm % (2 * CHUNK) == 0 and seq % tm == 0
    blk = lambda k: pl.BlockSpec((tm, d_ssm), lambda b, t: (b * nt + t, k))
    const = lambda b, t: (0, 0)
    side_specs = []
    for s in side:
        rb = _slab_rows(s.shape[0], batch * nt)
        last = s.shape[0] // rb - 1
        side_specs.append(pl.BlockSpec((rb, s.shape[1]),
                                       lambda b, t, last=last: (jnp.minimum(b * nt + t, last), 0)))
    outs = pl.pallas_call(
        functools.partial(_mixer_kernel, heads=heads, nside=len(side)),
        grid=(batch, nt),
        in_specs=[
            blk(1), blk(2),
            pl.BlockSpec((tm, LANES), lambda b, t: (b * nt + t, 0)),
            pl.BlockSpec((SSM_CONV, d_xbc), const),
            pl.BlockSpec((1, d_xbc), const),
            pl.BlockSpec((1, LANES), const),
            pl.BlockSpec((1, LANES), const),
            pl.BlockSpec((1, d_ssm), const),
        ] + side_specs,
        out_specs=[blk(0)] + side_specs,
        out_shape=[jax.ShapeDtypeStruct((T, d_ssm), BF16)] + [jax.ShapeDtypeStruct(s.shape, BF16) for s in side],
        scratch_shapes=[pltpu.VMEM((2, d_xbc // LANES, SUBLANES + CHUNK, LANES), F32),
                        pltpu.VMEM((2, d_xbc // LANES, CHUNK, LANES), F32),
                        pltpu.VMEM((G, N, d_ssm // G), F32),
                        pltpu.VMEM((SEL_TERMS * LANES, 2 * d_ssm), BF16)],
        compiler_params=_cparams(("arbitrary", "arbitrary")),
        name="mixer",
    )(proj, proj, dt_raw, conv_w, conv_b, dt_bias, a_log, d_exp, *side)
    return outs[0], outs[1:]


def _causal_conv_rows(u, prev, w, width):
    rows, C = u.shape
    nb = rows // SUBLANES
    full = jnp.concatenate([prev, u], axis=0).reshape(nb + 1, SUBLANES, C)
    sub = lax.broadcasted_iota(jnp.int32, (nb, SUBLANES, C), 1)
    tap = lambda k: jnp.broadcast_to(w[k:k + 1, :], (SUBLANES, C)).reshape(1, SUBLANES, C)
    y = full[1:] * tap(width - 1)
    for k in range(width - 1):
        s = width - 1 - k
        r = pltpu.roll(full, s, 1)
        y = y + jnp.where(sub >= s, r[1:], r[:-1]) * tap(k)
    return y.reshape(rows, C)


def _out_proj_kernel(y_ref, z_ref, ng_ref, gb_ref, gc_ref, u_ref, scw_ref, w_ref, x_ref, h_ref, carry, *, tn,
                     tiles_per_seq):
    tm, Ka = y_ref.shape
    cols = [slice(s * tn, (s + 1) * tn) for s in range(h_ref.shape[1] // tn)]
    p = gc_ref[...].astype(F32) * u_ref[...].astype(F32)
    prev = jnp.where((pl.program_id(0) % tiles_per_seq) == 0, 0.0, carry[...])
    carry[...] = p[tm - SUBLANES:, :]
    yb = (gb_ref[...].astype(F32) * _causal_conv_rows(p, prev, scw_ref[...], SHORT_CONV)).astype(BF16)
    for sl in cols:
        h_ref[:, sl] = x_ref[:, sl] + jnp.dot(yb, w_ref[Ka:, sl], preferred_element_type=F32)
    yg = y_ref[...].astype(F32) * _silu(z_ref[...].astype(F32))
    ya = (yg * _rms_scale(yg) * ng_ref[...]).astype(BF16)
    for sl in cols:
        h_ref[:, sl] += jnp.dot(ya, w_ref[:Ka, sl], preferred_element_type=F32)


def _out_proj(y, proj, norm_g, col0, sc_conv_w, w_out, x, seq, tm, tn):
    T, D = x.shape
    Ka = y.shape[1]
    Kb = sc_conv_w.shape[1]
    assert seq % tm == 0
    blk = lambda k: pl.BlockSpec((tm, Kb), lambda i: (i, col0 + k))
    return pl.pallas_call(
        functools.partial(_out_proj_kernel, tn=tn, tiles_per_seq=seq // tm),
        grid=(T // tm,),
        in_specs=[
            pl.BlockSpec((tm, Ka), lambda i: (i, 0)),
            pl.BlockSpec((tm, Ka), lambda i: (i, 0)),
            pl.BlockSpec((1, Ka), lambda i: (0, 0)),
            blk(0), blk(1), blk(2),
            pl.BlockSpec((SHORT_CONV, Kb), lambda i: (0, 0)),
            pl.BlockSpec((Ka + Kb, D), lambda i: (0, 0), pipeline_mode=pl.Buffered(1)),
            pl.BlockSpec((tm, D), lambda i: (i, 0)),
        ],
        out_specs=pl.BlockSpec((tm, D), lambda i: (i, 0)),
        out_shape=jax.ShapeDtypeStruct((T, D), F32),
        scratch_shapes=[pltpu.VMEM((SUBLANES, Kb), F32)],
        compiler_params=_cparams(("arbitrary",)),
        name="out_proj",
    )(y, proj, norm_g, proj, proj, proj, sc_conv_w, w_out, x)


def _ffn_kernel(h_ref, g_ref, wg_ref, wu_ref, wd_ref, gf_ref, o_ref, n_scr, *, final_norm, nsplit):
    f = pl.program_id(1)

    @pl.when(f == 0)
    def _():
        h = h_ref[...]
        n_scr[...] = (h * _rms_scale(h) * g_ref[...]).astype(BF16)
        o_ref[...] = h

    n = n_scr[...]
    gate = jnp.dot(n, wg_ref[...], preferred_element_type=F32)
    up = jnp.dot(n, wu_ref[...], preferred_element_type=F32)
    a = (_silu(gate) * up).astype(BF16)
    wn = o_ref.shape[1] // nsplit
    for s in range(nsplit):
        sl = slice(s * wn, (s + 1) * wn)
        o_ref[:, sl] += jnp.dot(a, wd_ref[:, sl], preferred_element_type=F32)

    if final_norm:
        @pl.when(f == pl.num_programs(1) - 1)
        def _():
            h2 = o_ref[...]
            o_ref[...] = h2 * _rms_scale(h2) * gf_ref[...]


def _ffn(h1, g, w_gate, w_up, w_down, g_final, final_norm, tm, tf):
    T, D = h1.shape
    F = w_gate.shape[1]
    return pl.pallas_call(
        functools.partial(_ffn_kernel, final_norm=final_norm, nsplit=4),
        grid=(T // tm, F // tf),
        in_specs=[
            pl.BlockSpec((tm, D), lambda i, f: (i, 0)),
            pl.BlockSpec((1, D), lambda i, f: (0, 0)),
            pl.BlockSpec((D, tf), lambda i, f: (0, f)),
            pl.BlockSpec((D, tf), lambda i, f: (0, f)),
            pl.BlockSpec((tf, D), lambda i, f: (f, 0)),
            pl.BlockSpec((1, D), lambda i, f: (0, 0)),
        ],
        out_specs=pl.BlockSpec((tm, D), lambda i, f: (i, 0)),
        out_shape=jax.ShapeDtypeStruct((T, D), F32),
        scratch_shapes=[pltpu.VMEM((tm, D), BF16)],
        compiler_params=_cparams(("parallel", "arbitrary")),
        name="ffn",
    )(h1, g, w_gate, w_up, w_down, g_final)


def _pad_lanes(v):
    return jnp.pad(v.reshape(1, -1), ((0, 0), (0, LANES - v.shape[-1])))


def kernel(x, norm_mix_g, w_in, ssm_conv_w, ssm_conv_b, ssm_dt_bias, ssm_A_log, ssm_D, ssm_norm_g,
           sc_conv_w, w_out, norm_ffn_g, w_gate, w_up, w_down, norm_final_g):
    batch, seq, d_model = x.shape
    depth = w_in.shape[0]
    d_ssm = ssm_norm_g.shape[1]
    d_xbc = ssm_conv_w.shape[2]
    heads = ssm_dt_bias.shape[1]
    d_conv = sc_conv_w.shape[2]
    assert d_ssm == d_conv == d_model and heads * SSM_HEADDIM == d_ssm and 3 * heads <= LANES
    assert seq % 1024 == 0
    off_dt = d_ssm + d_xbc
    off_cb = off_dt + heads
    assert w_in.shape[2] - off_cb == off_dt

    h = x.reshape(batch * seq, d_model)
    for l in range(depth):
        w_proj, w_dt = _prep_w_in(w_in[l].T, off_dt, heads, tn=512)
        proj, dt_raw = _in_proj(h, norm_mix_g[l].reshape(1, -1), w_proj, w_dt, tm=1024, tn=2048)
        y, (wo_b, wg_b, wu_b, wd_b) = _mixer(
            proj, dt_raw, ssm_conv_w[l], ssm_conv_b[l].reshape(1, -1), _pad_lanes(ssm_dt_bias[l]),
            _pad_lanes(ssm_A_log[l]), jnp.repeat(ssm_D[l], SSM_HEADDIM).reshape(1, -1),
            (w_out[l], w_gate[l], w_up[l], w_down[l]), batch, seq, heads, tm=256)
        h1 = _out_proj(y, proj, ssm_norm_g[l].reshape(1, -1), (d_ssm + d_xbc) // d_conv, sc_conv_w[l], wo_b, h,
                       seq, tm=512, tn=512)
        last = l == depth - 1
        h = _ffn(h1, norm_ffn_g[l].reshape(1, -1), wg_b, wu_b, wd_b, norm_final_g.reshape(1, -1),
                 final_norm=last, tm=1024, tf=512)
    return h.reshape(batch, seq, d_model)
```

```python
import functools

import jax
import jax.numpy as jnp
from jax import lax
from jax.experimental import pallas as pl
from jax.experimental.pallas import tpu as pltpu

F32 = jnp.float32
BF16 = jnp.bfloat16

EPS = 1e-5
LOG2E = 1.4426950408889634
SSM_HEADDIM = 64
SSM_GROUPS = 8
SSM_STATE = 128
SSM_CONV = 4
SHORT_CONV = 3
CHUNK = 128
SUBLANES = 8
LANES = 128
BF16_ROWS = 16
PHASES = 4
SEL_TERMS = 2
VMEM_LIMIT = 60 * 1024 * 1024


def _cparams(sem):
    return pltpu.CompilerParams(dimension_semantics=sem, vmem_limit_bytes=VMEM_LIMIT)


def _rms_scale(x):
    return lax.rsqrt(jnp.mean(x * x, axis=-1, keepdims=True) + EPS)


def _silu_from_half(h):
    return h + h * jnp.tanh(h)


def _silu(x):
    return _silu_from_half(0.5 * x)


def _split3(v):
    hi = v.astype(BF16)
    r1 = v - hi.astype(F32)
    mid = r1.astype(BF16)
    lo = (r1 - mid.astype(F32)).astype(BF16)
    return hi, mid, lo


def _prep_kernel(blk_ref, nxt_ref, w_ref, wdt_ref, *, nt, shift):
    k = pl.program_id(0)
    tn = blk_ref.shape[0]

    @pl.when(k < nt)
    def _():
        w_ref[...] = blk_ref[...].astype(BF16)

    @pl.when(k >= nt)
    def _():
        w_ref[:tn - shift, :] = blk_ref[shift:, :].astype(BF16)
        w_ref[tn - shift:, :] = nxt_ref[...].astype(BF16)

    @pl.when(k == nt)
    def _():
        wdt_ref[:shift, :] = blk_ref[:shift, :].astype(BF16)
        wdt_ref[shift:, :] = jnp.zeros((LANES - shift, wdt_ref.shape[1]), BF16)


def _prep_w_in(wt, off_b, shift, tn):
    D = wt.shape[1]
    nt = off_b // tn
    assert off_b % tn == 0 and tn % shift == 0 and shift % BF16_ROWS == 0 and wt.shape[0] == 2 * off_b + shift
    return pl.pallas_call(
        functools.partial(_prep_kernel, nt=nt, shift=shift),
        grid=(2 * nt,),
        in_specs=[
            pl.BlockSpec((tn, D), lambda k: (k, 0)),
            pl.BlockSpec((shift, D), lambda k: ((k + 1) * (tn // shift), 0)),
        ],
        out_specs=[
            pl.BlockSpec((tn, D), lambda k: (k, 0)),
            pl.BlockSpec((LANES, D), lambda k: (0, 0)),
        ],
        out_shape=[jax.ShapeDtypeStruct((2 * off_b, D), BF16), jax.ShapeDtypeStruct((LANES, D), BF16)],
        compiler_params=_cparams(("arbitrary",)),
        name="prep_w_in",
    )(wt, wt)


def _in_proj_kernel(x_ref, g_ref, w_ref, wdt_ref, out_ref, dt_ref, n_scr, s_scr, *, ncol):
    def mm(lhs, wt):
        return lax.dot_general(lhs, wt, (((1,), (1,)), ((), ())), preferred_element_type=F32)

    @pl.when(pl.program_id(1) == 0)
    def _():
        x = x_ref[...]
        s_scr[...] = jnp.broadcast_to(_rms_scale(x), s_scr.shape)
        n = (x * g_ref[...]).astype(BF16)
        n_scr[...] = n
        dt_ref[...] = mm(n, wdt_ref[...]) * s_scr[...]

    scale = jnp.concatenate([s_scr[...]] * (ncol // LANES), axis=1)
    for s in range(out_ref.shape[1] // ncol):
        cols = slice(s * ncol, (s + 1) * ncol)
        out_ref[:, cols] = (mm(n_scr[...], w_ref[cols, :]) * scale).astype(BF16)


def _in_proj(x, g, w, w_dt, tm, tn):
    T, D = x.shape
    N = w.shape[0]
    return pl.pallas_call(
        functools.partial(_in_proj_kernel, ncol=min(tn, 1024)),
        grid=(T // tm, N // tn),
        in_specs=[
            pl.BlockSpec((tm, D), lambda i, j: (i, 0)),
            pl.BlockSpec((1, D), lambda i, j: (0, 0)),
            pl.BlockSpec((tn, D), lambda i, j: (j, 0)),
            pl.BlockSpec((LANES, D), lambda i, j: (0, 0)),
        ],
        out_specs=[
            pl.BlockSpec((tm, tn), lambda i, j: (i, j)),
            pl.BlockSpec((tm, LANES), lambda i, j: (i, 0)),
        ],
        out_shape=[jax.ShapeDtypeStruct((T, N), BF16), jax.ShapeDtypeStruct((T, LANES), F32)],
        scratch_shapes=[pltpu.VMEM((tm, D), BF16), pltpu.VMEM((tm, LANES), F32)],
        compiler_params=_cparams(("parallel", "arbitrary")),
        name="in_proj",
    )(x, g, w, w_dt)


def _slab_rows(rows, steps):
    per = -(-rows // steps)
    per = -(-per // BF16_ROWS) * BF16_ROWS
    while rows % per:
        per += BF16_ROWS
    return per


def _conv_phases(buf, c, w, width, rows):
    n = rows // PHASES
    shifted = {m: buf[c, pl.ds(SUBLANES + m, n, stride=PHASES), :] for m in range(1 - width, PHASES)}
    out = []
    for p in range(PHASES):
        acc = None
        for k in range(width):
            term = w[k:k + 1, :] * shifted[p - (width - 1) + k]
            acc = term if acc is None else acc + term
        out.append(acc)
    return out


def _store_phases(dst, c, phases, rows):
    for p, v in enumerate(phases):
        dst[c, pl.ds(p, rows // PHASES, stride=PHASES), :] = v


def _mixer_kernel(*refs, heads, nside):
    xs_ref, bc_ref, dt_ref, cw_ref, cb_ref, dtb_ref, alog_ref, dexp_ref = refs[:8]
    side_in = refs[8:8 + nside]
    y_ref = refs[8 + nside]
    side_out = refs[9 + nside:9 + 2 * nside]
    bufs, acts, state, sel = refs[9 + 2 * nside:]
    L = CHUNK
    for src, dst in zip(side_in, side_out):
        dst[...] = src[...].astype(BF16)
    G, N, P = SSM_GROUPS, SSM_STATE, SSM_HEADDIM
    R = heads // G
    GW = R * P
    nx = heads * P // LANES
    lane = lambda c: slice(c * LANES, (c + 1) * LANES)

    @pl.when(pl.program_id(1) == 0)
    def _():
        bufs[0, :, 0:SUBLANES, :] = jnp.zeros((bufs.shape[1], SUBLANES, LANES), F32)
        state[...] = jnp.zeros(state.shape, F32)
        src_lane = lax.broadcasted_iota(jnp.int32, (LANES, 2 * heads * P), 0)
        out_col = lax.broadcasted_iota(jnp.int32, (LANES, 2 * heads * P), 1)
        grp = out_col // (2 * GW)
        kind = (out_col // GW) % 2
        head = grp * R + (out_col % GW) // P
        one = (src_lane == heads * (1 + kind) + head).astype(BF16)
        for k in range(SEL_TERMS):
            sel[k * LANES:(k + 1) * LANES, :] = one

    def chunk(ci, parity):
        buf, act, buf_next = bufs.at[parity], acts.at[parity], bufs.at[1 - parity]
        rows = pl.ds(pl.multiple_of(ci * L, L), L)

        dt_in = (dt_ref[rows, :] + dtb_ref[...]).T[:heads]
        dt = jnp.maximum(dt_in, 0.0) + jnp.log1p(jnp.exp(-jnp.abs(dt_in)))
        a2 = jnp.broadcast_to(-LOG2E * jnp.exp(alog_ref[...]), (L, LANES)).T[:heads]
        dA2 = dt * a2
        row = lax.broadcasted_iota(jnp.int32, (L, L), 0)
        col = lax.broadcasted_iota(jnp.int32, (L, L), 1)
        causal = row >= col
        triu = (row <= col).astype(BF16)
        cs2 = jnp.dot(jnp.concatenate(_split3(dA2), axis=1), jnp.concatenate([triu, triu, triu], axis=0),
                      preferred_element_type=F32)
        cs2_last = cs2[:, L - 1:L]
        ecs = jnp.exp2(cs2)
        wdec = dt * jnp.exp2(cs2_last - cs2)
        csd = cs2 - jnp.log2(dt)
        colT = jnp.concatenate([cs2, wdec, ecs, jnp.zeros((LANES - 3 * heads, L), F32)], axis=0).T
        colT3 = jnp.concatenate(_split3(colT)[:SEL_TERMS], axis=1)

        for c in range(buf.shape[0]):
            src = xs_ref[rows, lane(c)] if c < nx else bc_ref[rows, lane(c - nx)]
            buf[c, SUBLANES:, :] = src.astype(F32)
        for c in range(buf.shape[0]):
            halves = _conv_phases(buf, c, 0.5 * cw_ref[:, lane(c)], SSM_CONV, L)
            b = 0.5 * cb_ref[:, lane(c)]
            _store_phases(act, c, [_silu_from_half(v + b) for v in halves], L)
            buf_next[c, 0:SUBLANES, :] = buf[c, L:L + SUBLANES, :]

        head_of_lane = lax.broadcasted_iota(jnp.int32, (L, GW), 1) // P
        for g in range(G):
            sl = slice(g * GW, (g + 1) * GW)
            xs = jnp.concatenate([act[g * GW // LANES + k] for k in range(GW // LANES)], axis=1)
            xs_b = xs.astype(BF16)
            Bg = act[nx + g].astype(BF16)
            Cg = act[nx + G + g].astype(BF16)
            CB = lax.dot_general(Cg, Bg, (((1,), (1,)), ((), ())), preferred_element_type=F32)
            Hs = state[g]
            y_off = jnp.dot(Cg, Hs.astype(BF16), preferred_element_type=F32)
            Ms, xbd = [], []
            for r in range(R):
                h = g * R + r
                seg = colT[:, h:h + 1] - csd[h:h + 1, :]
                Ms.append((CB * jnp.exp2(jnp.where(causal, seg, -jnp.inf))).astype(BF16))
                xbd.append(jnp.where(head_of_lane == r, xs_b, jnp.zeros_like(xs_b)))
            y_diag = jnp.dot(jnp.concatenate(Ms, axis=1), jnp.concatenate(xbd, axis=0),
                             preferred_element_type=F32)
            expd = jnp.dot(colT3, sel[:, 2 * g * GW:2 * (g + 1) * GW], preferred_element_type=F32)
            wdec_e = expd[:, :GW]
            ecs_e = expd[:, GW:]
            xt_b = (xs * wdec_e).astype(BF16)
            st_new = lax.dot_general(Bg, xt_b, (((0,), (0,)), ((), ())), preferred_element_type=F32)
            state[g] = Hs * ecs_e[L - 1:L, :] + st_new
            y_ref[rows, sl] = (y_diag + y_off * ecs_e + dexp_ref[:, sl] * xs).astype(BF16)

    def chunk_pair(pi, carry):
        chunk(2 * pi, 0)
        chunk(2 * pi + 1, 1)
        return carry

    lax.fori_loop(0, xs_ref.shape[0] // (2 * L), chunk_pair, 0)


def _mixer(proj, dt_raw, conv_w, conv_b, dt_bias, a_log, d_exp, side, batch, seq, heads, tm):
    T = proj.shape[0]
    nt = seq // tm
    G, N = SSM_GROUPS, SSM_STATE
    d_ssm = heads * SSM_HEADDIM
    d_xbc = conv_w.shape[1]
    assert d_xbc - d_ssm == 2 * G * N == d_ssm and tm % (2 * CHUNK) == 0 and seq % tm == 0
    blk = lambda k: pl.BlockSpec((tm, d_ssm), lambda b, t: (b * nt + t, k))
    const = lambda b, t: (0, 0)
    side_specs = []
    for s in side:
        rb = _slab_rows(s.shape[0], batch * nt)
        last = s.shape[0] // rb - 1
        side_specs.append(pl.BlockSpec((rb, s.shape[1]),
                                       lambda b, t, last=last: (jnp.minimum(b * nt + t, last), 0)))
    outs = pl.pallas_call(
        functools.partial(_mixer_kernel, heads=heads, nside=len(side)),
        grid=(batch, nt),
        in_specs=[
            blk(1), blk(2),
            pl.BlockSpec((tm, LANES), lambda b, t: (b * nt + t, 0)),
            pl.BlockSpec((SSM_CONV, d_xbc), const),
            pl.BlockSpec((1, d_xbc), const),
            pl.BlockSpec((1, LANES), const),
            pl.BlockSpec((1, LANES), const),
            pl.BlockSpec((1, d_ssm), const),
        ] + side_specs,
        out_specs=[blk(0)] + side_specs,
        out_shape=[jax.ShapeDtypeStruct((T, d_ssm), BF16)] + [jax.ShapeDtypeStruct(s.shape, BF16) for s in side],
        scratch_shapes=[pltpu.VMEM((2, d_xbc // LANES, SUBLANES + CHUNK, LANES), F32),
                        pltpu.VMEM((2, d_xbc // LANES, CHUNK, LANES), F32),
                        pltpu.VMEM((G, N, d_ssm // G), F32),
                        pltpu.VMEM((SEL_TERMS * LANES, 2 * d_ssm), BF16)],
        compiler_params=_cparams(("arbitrary", "arbitrary")),
        name="mixer",
    )(proj, proj, dt_raw, conv_w, conv_b, dt_bias, a_log, d_exp, *side)
    return outs[0], outs[1:]


def _causal_conv_rows(u, prev, w, width):
    rows, C = u.shape
    nb = rows // SUBLANES
    full = jnp.concatenate([prev, u], axis=0).reshape(nb + 1, SUBLANES, C)
    sub = lax.broadcasted_iota(jnp.int32, (nb, SUBLANES, C), 1)
    tap = lambda k: jnp.broadcast_to(w[k:k + 1, :], (SUBLANES, C)).reshape(1, SUBLANES, C)
    y = full[1:] * tap(width - 1)
    for k in range(width - 1):
        s = width - 1 - k
        r = pltpu.roll(full, s, 1)
        y = y + jnp.where(sub >= s, r[1:], r[:-1]) * tap(k)
    return y.reshape(rows, C)


def _out_proj_kernel(y_ref, z_ref, ng_ref, gb_ref, gc_ref, u_ref, scw_ref, w_ref, x_ref, h_ref, carry, *, tn,
                     tiles_per_seq):
    tm, Ka = y_ref.shape
    cols = [slice(s * tn, (s + 1) * tn) for s in range(h_ref.shape[1] // tn)]
    p = gc_ref[...].astype(F32) * u_ref[...].astype(F32)
    prev = jnp.where((pl.program_id(0) % tiles_per_seq) == 0, 0.0, carry[...])
    carry[...] = p[tm - SUBLANES:, :]
    yb = (gb_ref[...].astype(F32) * _causal_conv_rows(p, prev, scw_ref[...], SHORT_CONV)).astype(BF16)
    for sl in cols:
        h_ref[:, sl] = x_ref[:, sl] + jnp.dot(yb, w_ref[Ka:, sl], preferred_element_type=F32)
    yg = y_ref[...].astype(F32) * _silu(z_ref[...].astype(F32))
    ya = (yg * ng_ref[...]).astype(BF16)
    scale = jnp.broadcast_to(_rms_scale(yg), (tm, tn))
    for sl in cols:
        h_ref[:, sl] += scale * jnp.dot(ya, w_ref[:Ka, sl], preferred_element_type=F32)


def _out_proj(y, proj, norm_g, col0, sc_conv_w, w_out, x, seq, tm, tn):
    T, D = x.shape
    Ka = y.shape[1]
    Kb = sc_conv_w.shape[1]
    assert seq % tm == 0
    blk = lambda k: pl.BlockSpec((tm, Kb), lambda i: (i, col0 + k))
    return pl.pallas_call(
        functools.partial(_out_proj_kernel, tn=tn, tiles_per_seq=seq // tm),
        grid=(T // tm,),
        in_specs=[
            pl.BlockSpec((tm, Ka), lambda i: (i, 0)),
            pl.BlockSpec((tm, Ka), lambda i: (i, 0)),
            pl.BlockSpec((1, Ka), lambda i: (0, 0)),
            blk(0), blk(1), blk(2),
            pl.BlockSpec((SHORT_CONV, Kb), lambda i: (0, 0)),
            pl.BlockSpec((Ka + Kb, D), lambda i: (0, 0), pipeline_mode=pl.Buffered(1)),
            pl.BlockSpec((tm, D), lambda i: (i, 0)),
        ],
        out_specs=pl.BlockSpec((tm, D), lambda i: (i, 0)),
        out_shape=jax.ShapeDtypeStruct((T, D), F32),
        scratch_shapes=[pltpu.VMEM((SUBLANES, Kb), F32)],
        compiler_params=_cparams(("arbitrary",)),
        name="out_proj",
    )(y, proj, norm_g, proj, proj, proj, sc_conv_w, w_out, x)


def _ffn_kernel(h_ref, g_ref, wg_ref, wu_ref, wd_ref, gf_ref, o_ref, n_scr, *, final_norm, nsplit):
    f = pl.program_id(1)

    @pl.when(f == 0)
    def _():
        h = h_ref[...]
        n_scr[...] = (h * _rms_scale(h) * g_ref[...]).astype(BF16)
        o_ref[...] = h

    n = n_scr[...]
    gate = jnp.dot(n, wg_ref[...], preferred_element_type=F32)
    up = jnp.dot(n, wu_ref[...], preferred_element_type=F32)
    a = (_silu(gate) * up).astype(BF16)
    wn = o_ref.shape[1] // nsplit
    for s in range(nsplit):
        sl = slice(s * wn, (s + 1) * wn)
        o_ref[:, sl] += jnp.dot(a, wd_ref[:, sl], preferred_element_type=F32)

    if final_norm:
        @pl.when(f == pl.num_programs(1) - 1)
        def _():
            h2 = o_ref[...]
            o_ref[...] = h2 * _rms_scale(h2) * gf_ref[...]


def _ffn(h1, g, w_gate, w_up, w_down, g_final, final_norm, tm, tf):
    T, D = h1.shape
    F = w_gate.shape[1]
    return pl.pallas_call(
        functools.partial(_ffn_kernel, final_norm=final_norm, nsplit=4),
        grid=(T // tm, F // tf),
        in_specs=[
            pl.BlockSpec((tm, D), lambda i, f: (i, 0)),
            pl.BlockSpec((1, D), lambda i, f: (0, 0)),
            pl.BlockSpec((D, tf), lambda i, f: (0, f)),
            pl.BlockSpec((D, tf), lambda i, f: (0, f)),
            pl.BlockSpec((tf, D), lambda i, f: (f, 0)),
            pl.BlockSpec((1, D), lambda i, f: (0, 0)),
        ],
        out_specs=pl.BlockSpec((tm, D), lambda i, f: (i, 0)),
        out_shape=jax.ShapeDtypeStruct((T, D), F32),
        scratch_shapes=[pltpu.VMEM((tm, D), BF16)],
        compiler_params=_cparams(("parallel", "arbitrary")),
        name="ffn",
    )(h1, g, w_gate, w_up, w_down, g_final)


def _pad_lanes(v):
    return jnp.pad(v.reshape(1, -1), ((0, 0), (0, LANES - v.shape[-1])))


def kernel(x, norm_mix_g, w_in, ssm_conv_w, ssm_conv_b, ssm_dt_bias, ssm_A_log, ssm_D, ssm_norm_g,
           sc_conv_w, w_out, norm_ffn_g, w_gate, w_up, w_down, norm_final_g):
    batch, seq, d_model = x.shape
    depth = w_in.shape[0]
    d_ssm = ssm_norm_g.shape[1]
    d_xbc = ssm_conv_w.shape[2]
    heads = ssm_dt_bias.shape[1]
    d_conv = sc_conv_w.shape[2]
    assert d_ssm == d_conv == d_model and heads * SSM_HEADDIM == d_ssm and 3 * heads <= LANES
    assert seq % 1024 == 0
    off_dt = d_ssm + d_xbc
    off_cb = off_dt + heads
    assert w_in.shape[2] - off_cb == off_dt

    h = x.reshape(batch * seq, d_model)
    for l in range(depth):
        w_proj, w_dt = _prep_w_in(w_in[l].T, off_dt, heads, tn=512)
        proj, dt_raw = _in_proj(h, norm_mix_g[l].reshape(1, -1), w_proj, w_dt, tm=1024, tn=2048)
        y, (wo_b, wg_b, wu_b, wd_b) = _mixer(
            proj, dt_raw, ssm_conv_w[l], ssm_conv_b[l].reshape(1, -1), _pad_lanes(ssm_dt_bias[l]),
            _pad_lanes(ssm_A_log[l]), jnp.repeat(ssm_D[l], SSM_HEADDIM).reshape(1, -1),
            (w_out[l], w_gate[l], w_up[l], w_down[l]), batch, seq, heads, tm=256)
        h1 = _out_proj(y, proj, ssm_norm_g[l].reshape(1, -1), (d_ssm + d_xbc) // d_conv, sc_conv_w[l], wo_b, h,
                       seq, tm=512, tn=512)
        last = l == depth - 1
        h = _ffn(h1, norm_ffn_g[l].reshape(1, -1), wg_b, wu_b, wd_b, norm_final_g.reshape(1, -1),
                 final_norm=last, tm=1024, tf=512)
    return h.reshape(batch, seq, d_model)
```

```python
import functools

import jax
import jax.numpy as jnp
from jax import lax
from jax.experimental import pallas as pl
from jax.experimental.pallas import tpu as pltpu

F32 = jnp.float32
BF16 = jnp.bfloat16

EPS = 1e-5
LOG2E = 1.4426950408889634
SSM_HEADDIM = 64
SSM_GROUPS = 8
SSM_STATE = 128
SSM_CONV = 4
SHORT_CONV = 3
CHUNK = 128
SUBLANES = 8
LANES = 128
BF16_ROWS = 16
PHASES = 4
SEL_TERMS = 2
VMEM_LIMIT = 60 * 1024 * 1024


def _cparams(sem):
    return pltpu.CompilerParams(dimension_semantics=sem, vmem_limit_bytes=VMEM_LIMIT)


def _rms_scale(x):
    return lax.rsqrt(jnp.mean(x * x, axis=-1, keepdims=True) + EPS)


def _silu_from_half(h):
    return h + h * jnp.tanh(h)


def _silu(x):
    return _silu_from_half(0.5 * x)


def _split3(v):
    hi = v.astype(BF16)
    r1 = v - hi.astype(F32)
    mid = r1.astype(BF16)
    lo = (r1 - mid.astype(F32)).astype(BF16)
    return hi, mid, lo


def _prep_kernel(blk_ref, nxt_ref, w_ref, wdt_ref, *, nt, shift):
    k = pl.program_id(0)
    tn = blk_ref.shape[0]

    @pl.when(k < nt)
    def _():
        w_ref[...] = blk_ref[...].astype(BF16)

    @pl.when(k >= nt)
    def _():
        w_ref[:tn - shift, :] = blk_ref[shift:, :].astype(BF16)
        w_ref[tn - shift:, :] = nxt_ref[...].astype(BF16)

    @pl.when(k == nt)
    def _():
        wdt_ref[:shift, :] = blk_ref[:shift, :].astype(BF16)
        wdt_ref[shift:, :] = jnp.zeros((LANES - shift, wdt_ref.shape[1]), BF16)


def _prep_w_in(wt, off_b, shift, tn):
    D = wt.shape[1]
    nt = off_b // tn
    assert off_b % tn == 0 and tn % shift == 0 and shift % BF16_ROWS == 0 and wt.shape[0] == 2 * off_b + shift
    return pl.pallas_call(
        functools.partial(_prep_kernel, nt=nt, shift=shift),
        grid=(2 * nt,),
        in_specs=[
            pl.BlockSpec((tn, D), lambda k: (k, 0)),
            pl.BlockSpec((shift, D), lambda k: ((k + 1) * (tn // shift), 0)),
        ],
        out_specs=[
            pl.BlockSpec((tn, D), lambda k: (k, 0)),
            pl.BlockSpec((LANES, D), lambda k: (0, 0)),
        ],
        out_shape=[jax.ShapeDtypeStruct((2 * off_b, D), BF16), jax.ShapeDtypeStruct((LANES, D), BF16)],
        compiler_params=_cparams(("arbitrary",)),
        name="prep_w_in",
    )(wt, wt)


def _in_proj_kernel(x_ref, g_ref, w_ref, wdt_ref, out_ref, dt_ref, n_scr, s_scr, *, ncol):
    def mm(lhs, wt):
        return lax.dot_general(lhs, wt, (((1,), (1,)), ((), ())), preferred_element_type=F32)

    @pl.when(pl.program_id(1) == 0)
    def _():
        x = x_ref[...]
        s_scr[...] = jnp.broadcast_to(_rms_scale(x), s_scr.shape)
        n = (x * g_ref[...]).astype(BF16)
        n_scr[...] = n
        dt_ref[...] = mm(n, wdt_ref[...]) * s_scr[...]

    scale = jnp.concatenate([s_scr[...]] * (ncol // LANES), axis=1)
    for s in range(out_ref.shape[1] // ncol):
        cols = slice(s * ncol, (s + 1) * ncol)
        out_ref[:, cols] = (mm(n_scr[...], w_ref[cols, :]) * scale).astype(BF16)


def _in_proj(x, g, w, w_dt, tm, tn):
    T, D = x.shape
    N = w.shape[0]
    return pl.pallas_call(
        functools.partial(_in_proj_kernel, ncol=min(tn, 1024)),
        grid=(T // tm, N // tn),
        in_specs=[
            pl.BlockSpec((tm, D), lambda i, j: (i, 0)),
            pl.BlockSpec((1, D), lambda i, j: (0, 0)),
            pl.BlockSpec((tn, D), lambda i, j: (j, 0)),
            pl.BlockSpec((LANES, D), lambda i, j: (0, 0)),
        ],
        out_specs=[
            pl.BlockSpec((tm, tn), lambda i, j: (i, j)),
            pl.BlockSpec((tm, LANES), lambda i, j: (i, 0)),
        ],
        out_shape=[jax.ShapeDtypeStruct((T, N), BF16), jax.ShapeDtypeStruct((T, LANES), F32)],
        scratch_shapes=[pltpu.VMEM((tm, D), BF16), pltpu.VMEM((tm, LANES), F32)],
        compiler_params=_cparams(("parallel", "arbitrary")),
        name="in_proj",
    )(x, g, w, w_dt)


def _slab_rows(rows, steps):
    per = -(-rows // steps)
    per = -(-per // BF16_ROWS) * BF16_ROWS
    while rows % per:
        per += BF16_ROWS
    return per


def _conv_phases(buf, c, w, width, rows):
    n = rows // PHASES
    shifted = {m: buf[c, pl.ds(SUBLANES + m, n, stride=PHASES), :] for m in range(1 - width, PHASES)}
    out = []
    for p in range(PHASES):
        acc = None
        for k in range(width):
            term = w[k:k + 1, :] * shifted[p - (width - 1) + k]
            acc = term if acc is None else acc + term
        out.append(acc)
    return out


def _store_phases(dst, c, phases, rows):
    for p, v in enumerate(phases):
        dst[c, pl.ds(p, rows // PHASES, stride=PHASES), :] = v


def _mixer_kernel(*refs, heads, nside):
    xs_ref, bc_ref, dt_ref, cw_ref, cb_ref, dtb_ref, alog_ref, dexp_ref = refs[:8]
    side_in = refs[8:8 + nside]
    y_ref = refs[8 + nside]
    side_out = refs[9 + nside:9 + 2 * nside]
    bufs, acts, state, sel = refs[9 + 2 * nside:]
    L = CHUNK
    for src, dst in zip(side_in, side_out):
        dst[...] = src[...].astype(BF16)
    G, N, P = SSM_GROUPS, SSM_STATE, SSM_HEADDIM
    R = heads // G
    GW = R * P
    nx = heads * P // LANES
    lane = lambda c: slice(c * LANES, (c + 1) * LANES)

    @pl.when(pl.program_id(1) == 0)
    def _():
        bufs[0, :, 0:SUBLANES, :] = jnp.zeros((bufs.shape[1], SUBLANES, LANES), F32)
        state[...] = jnp.zeros(state.shape, F32)
        src_lane = lax.broadcasted_iota(jnp.int32, (LANES, 2 * heads * P), 0)
        out_col = lax.broadcasted_iota(jnp.int32, (LANES, 2 * heads * P), 1)
        grp = out_col // (2 * GW)
        kind = (out_col // GW) % 2
        head = grp * R + (out_col % GW) // P
        one = (src_lane == heads * (1 + kind) + head).astype(BF16)
        for k in range(SEL_TERMS):
            sel[k * LANES:(k + 1) * LANES, :] = one

    def chunk(ci, parity):
        buf, act, buf_next = bufs.at[parity], acts.at[parity], bufs.at[1 - parity]
        rows = pl.ds(pl.multiple_of(ci * L, L), L)

        dt_in = (dt_ref[rows, :] + dtb_ref[...]).T[:heads]
        dt = jnp.maximum(dt_in, 0.0) + jnp.log1p(jnp.exp(-jnp.abs(dt_in)))
        a2 = jnp.broadcast_to(-LOG2E * jnp.exp(alog_ref[...]), (L, LANES)).T[:heads]
        dA2 = dt * a2
        row = lax.broadcasted_iota(jnp.int32, (L, L), 0)
        col = lax.broadcasted_iota(jnp.int32, (L, L), 1)
        causal = row >= col
        triu = (row <= col).astype(BF16)
        cs2 = jnp.dot(jnp.concatenate(_split3(dA2), axis=1), jnp.concatenate([triu, triu, triu], axis=0),
                      preferred_element_type=F32)
        cs2_last = cs2[:, L - 1:L]
        ecs = jnp.exp2(cs2)
        wdec = dt * jnp.exp2(cs2_last - cs2)
        csd = cs2 - jnp.log2(dt)
        colT = jnp.concatenate([cs2, wdec, ecs, jnp.zeros((LANES - 3 * heads, L), F32)], axis=0).T
        colT3 = jnp.concatenate(_split3(colT)[:SEL_TERMS], axis=1)

        for c in range(buf.shape[0]):
            src = xs_ref[rows, lane(c)] if c < nx else bc_ref[rows, lane(c - nx)]
            buf[c, SUBLANES:, :] = src.astype(F32)
        for c in range(buf.shape[0]):
            halves = _conv_phases(buf, c, 0.5 * cw_ref[:, lane(c)], SSM_CONV, L)
            b = 0.5 * cb_ref[:, lane(c)]
            _store_phases(act, c, [_silu_from_half(v + b) for v in halves], L)
            buf_next[c, 0:SUBLANES, :] = buf[c, L:L + SUBLANES, :]

        head_of_lane = lax.broadcasted_iota(jnp.int32, (L, GW), 1) // P
        for g in range(G):
            sl = slice(g * GW, (g + 1) * GW)
            xs = jnp.concatenate([act[g * GW // LANES + k] for k in range(GW // LANES)], axis=1)
            xs_b = xs.astype(BF16)
            Bg = act[nx + g].astype(BF16)
            Cg = act[nx + G + g].astype(BF16)
            CB = lax.dot_general(Cg, Bg, (((1,), (1,)), ((), ())), preferred_element_type=F32)
            Hs = state[g]
            y_off = jnp.dot(Cg, Hs.astype(BF16), preferred_element_type=F32)
            Ms, xbd = [], []
            for r in range(R):
                h = g * R + r
                seg = colT[:, h:h + 1] - csd[h:h + 1, :]
                Ms.append((CB * jnp.exp2(jnp.where(causal, seg, -jnp.inf))).astype(BF16))
                xbd.append(jnp.where(head_of_lane == r, xs_b, jnp.zeros_like(xs_b)))
            y_diag = jnp.dot(jnp.concatenate(Ms, axis=1), jnp.concatenate(xbd, axis=0),
                             preferred_element_type=F32)
            expd = jnp.dot(colT3, sel[:, 2 * g * GW:2 * (g + 1) * GW], preferred_element_type=F32)
            wdec_e = expd[:, :GW]
            ecs_e = expd[:, GW:]
            xt_b = (xs * wdec_e).astype(BF16)
            st_new = lax.dot_general(Bg, xt_b, (((0,), (0,)), ((), ())), preferred_element_type=F32)
            state[g] = Hs * ecs_e[L - 1:L, :] + st_new
            y_ref[rows, sl] = (y_diag + y_off * ecs_e + dexp_ref[:, sl] * xs).astype(BF16)

    def chunk_pair(pi, carry):
        chunk(2 * pi, 0)
        chunk(2 * pi + 1, 1)
        return carry

    lax.fori_loop(0, xs_ref.shape[0] // (2 * L), chunk_pair, 0)


def _mixer(proj, dt_raw, conv_w, conv_b, dt_bias, a_log, d_exp, side, batch, seq, heads, tm):
    T = proj.shape[0]
    nt = seq // tm
    G, N = SSM_GROUPS, SSM_STATE
    d_ssm = heads * SSM_HEADDIM
    d_xbc = conv_w.shape[1]
    assert d_xbc - d_ssm == 2 * G * N == d_ssm and tm % (2 * CHUNK) == 0 and seq % tm == 0
    blk = lambda k: pl.BlockSpec((tm, d_ssm), lambda b, t: (b * nt + t, k))
    const = lambda b, t: (0, 0)
    side_specs = []
    for s in side:
        rb = _slab_rows(s.shape[0], batch * nt)
        last = s.shape[0] // rb - 1
        side_specs.append(pl.BlockSpec((rb, s.shape[1]),
                                       lambda b, t, last=last: (jnp.minimum(b * nt + t, last), 0)))
    outs = pl.pallas_call(
        functools.partial(_mixer_kernel, heads=heads, nside=len(side)),
        grid=(batch, nt),
        in_specs=[
            blk(1), blk(2),
            pl.BlockSpec((tm, LANES), lambda b, t: (b * nt + t, 0)),
            pl.BlockSpec((SSM_CONV, d_xbc), const),
            pl.BlockSpec((1, d_xbc), const),
            pl.BlockSpec((1, LANES), const),
            pl.BlockSpec((1, LANES), const),
            pl.BlockSpec((1, d_ssm), const),
        ] + side_specs,
        out_specs=[blk(0)] + side_specs,
        out_shape=[jax.ShapeDtypeStruct((T, d_ssm), BF16)] + [jax.ShapeDtypeStruct(s.shape, BF16) for s in side],
        scratch_shapes=[pltpu.VMEM((2, d_xbc // LANES, SUBLANES + CHUNK, LANES), F32),
                        pltpu.VMEM((2, d_xbc // LANES, CHUNK, LANES), F32),
                        pltpu.VMEM((G, N, d_ssm // G), F32),
                        pltpu.VMEM((SEL_TERMS * LANES, 2 * d_ssm), BF16)],
        compiler_params=_cparams(("arbitrary", "arbitrary")),
        name="mixer",
    )(proj, proj, dt_raw, conv_w, conv_b, dt_bias, a_log, d_exp, *side)
    return outs[0], outs[1:]


def _causal_conv_rows(u, prev, w, width):
    rows, C = u.shape
    nb = rows // SUBLANES
    full = jnp.concatenate([prev, u], axis=0).reshape(nb + 1, SUBLANES, C)
    sub = lax.broadcasted_iota(jnp.int32, (nb, SUBLANES, C), 1)
    tap = lambda k: jnp.broadcast_to(w[k:k + 1, :], (SUBLANES, C)).reshape(1, SUBLANES, C)
    y = full[1:] * tap(width - 1)
    for k in range(width - 1):
        s = width - 1 - k
        r = pltpu.roll(full, s, 1)
        y = y + jnp.where(sub >= s, r[1:], r[:-1]) * tap(k)
    return y.reshape(rows, C)


def _out_proj_kernel(y_ref, z_ref, ng_ref, gb_ref, gc_ref, u_ref, scw_ref, w_ref, x_ref, h_ref, carry, *, tn,
                     tiles_per_seq):
    tm, Ka = y_ref.shape
    cols = [slice(s * tn, (s + 1) * tn) for s in range(h_ref.shape[1] // tn)]
    p = gc_ref[...].astype(F32) * u_ref[...].astype(F32)
    prev = jnp.where((pl.program_id(0) % tiles_per_seq) == 0, 0.0, carry[...])
    carry[...] = p[tm - SUBLANES:, :]
    yb = (gb_ref[...].astype(F32) * _causal_conv_rows(p, prev, scw_ref[...], SHORT_CONV)).astype(BF16)
    for sl in cols:
        h_ref[:, sl] = x_ref[:, sl] + jnp.dot(yb, w_ref[Ka:, sl], preferred_element_type=F32)
    yg = y_ref[...].astype(F32) * _silu(z_ref[...].astype(F32))
    ya = (yg * ng_ref[...]).astype(BF16)
    scale = jnp.broadcast_to(_rms_scale(yg), (tm, tn))
    for sl in cols:
        h_ref[:, sl] += scale * jnp.dot(ya, w_ref[:Ka, sl], preferred_element_type=F32)


def _out_proj(y, proj, norm_g, col0, sc_conv_w, w_out, x, seq, tm, tn):
    T, D = x.shape
    Ka = y.shape[1]
    Kb = sc_conv_w.shape[1]
    assert seq % tm == 0
    blk = lambda k: pl.BlockSpec((tm, Kb), lambda i: (i, col0 + k))
    return pl.pallas_call(
        functools.partial(_out_proj_kernel, tn=tn, tiles_per_seq=seq // tm),
        grid=(T // tm,),
        in_specs=[
            pl.BlockSpec((tm, Ka), lambda i: (i, 0)),
            pl.BlockSpec((tm, Ka), lambda i: (i, 0)),
            pl.BlockSpec((1, Ka), lambda i: (0, 0)),
            blk(0), blk(1), blk(2),
            pl.BlockSpec((SHORT_CONV, Kb), lambda i: (0, 0)),
            pl.BlockSpec((Ka + Kb, D), lambda i: (0, 0), pipeline_mode=pl.Buffered(1)),
            pl.BlockSpec((tm, D), lambda i: (i, 0)),
        ],
        out_specs=pl.BlockSpec((tm, D), lambda i: (i, 0)),
        out_shape=jax.ShapeDtypeStruct((T, D), F32),
        scratch_shapes=[pltpu.VMEM((SUBLANES, Kb), F32)],
        compiler_params=_cparams(("arbitrary",)),
        name="out_proj",
    )(y, proj, norm_g, proj, proj, proj, sc_conv_w, w_out, x)


def _ffn_kernel(h_ref, g_ref, wg_ref, wu_ref, wd_ref, gf_ref, o_ref, n_scr, s_scr, *, final_norm, nsplit):
    f = pl.program_id(1)

    @pl.when(f == 0)
    def _():
        h = h_ref[...]
        s_scr[...] = jnp.broadcast_to(_rms_scale(h), s_scr.shape)
        n_scr[...] = (h * g_ref[...]).astype(BF16)
        o_ref[...] = h

    n = n_scr[...]
    scale = jnp.concatenate([s_scr[...]] * (wg_ref.shape[1] // LANES), axis=1)
    gate = jnp.dot(n, wg_ref[...], preferred_element_type=F32) * scale
    up = jnp.dot(n, wu_ref[...], preferred_element_type=F32) * scale
    a = (_silu(gate) * up).astype(BF16)
    wn = o_ref.shape[1] // nsplit
    for s in range(nsplit):
        sl = slice(s * wn, (s + 1) * wn)
        o_ref[:, sl] += jnp.dot(a, wd_ref[:, sl], preferred_element_type=F32)

    if final_norm:
        @pl.when(f == pl.num_programs(1) - 1)
        def _():
            h2 = o_ref[...]
            o_ref[...] = h2 * _rms_scale(h2) * gf_ref[...]


def _ffn(h1, g, w_gate, w_up, w_down, g_final, final_norm, tm, tf):
    T, D = h1.shape
    F = w_gate.shape[1]
    return pl.pallas_call(
        functools.partial(_ffn_kernel, final_norm=final_norm, nsplit=4),
        grid=(T // tm, F // tf),
        in_specs=[
            pl.BlockSpec((tm, D), lambda i, f: (i, 0)),
            pl.BlockSpec((1, D), lambda i, f: (0, 0)),
            pl.BlockSpec((D, tf), lambda i, f: (0, f)),
            pl.BlockSpec((D, tf), lambda i, f: (0, f)),
            pl.BlockSpec((tf, D), lambda i, f: (f, 0)),
            pl.BlockSpec((1, D), lambda i, f: (0, 0)),
        ],
        out_specs=pl.BlockSpec((tm, D), lambda i, f: (i, 0)),
        out_shape=jax.ShapeDtypeStruct((T, D), F32),
        scratch_shapes=[pltpu.VMEM((tm, D), BF16), pltpu.VMEM((tm, LANES), F32)],
        compiler_params=_cparams(("parallel", "arbitrary")),
        name="ffn",
    )(h1, g, w_gate, w_up, w_down, g_final)


def _pad_lanes(v):
    return jnp.pad(v.reshape(1, -1), ((0, 0), (0, LANES - v.shape[-1])))


def kernel(x, norm_mix_g, w_in, ssm_conv_w, ssm_conv_b, ssm_dt_bias, ssm_A_log, ssm_D, ssm_norm_g,
           sc_conv_w, w_out, norm_ffn_g, w_gate, w_up, w_down, norm_final_g):
    batch, seq, d_model = x.shape
    depth = w_in.shape[0]
    d_ssm = ssm_norm_g.shape[1]
    d_xbc = ssm_conv_w.shape[2]
    heads = ssm_dt_bias.shape[1]
    d_conv = sc_conv_w.shape[2]
    assert d_ssm == d_conv == d_model and heads * SSM_HEADDIM == d_ssm and 3 * heads <= LANES
    assert seq % 1024 == 0
    off_dt = d_ssm + d_xbc
    off_cb = off_dt + heads
    assert w_in.shape[2] - off_cb == off_dt

    h = x.reshape(batch * seq, d_model)
    for l in range(depth):
        w_proj, w_dt = _prep_w_in(w_in[l].T, off_dt, heads, tn=512)
        proj, dt_raw = _in_proj(h, norm_mix_g[l].reshape(1, -1), w_proj, w_dt, tm=1024, tn=2048)
        y, (wo_b, wg_b, wu_b, wd_b) = _mixer(
            proj, dt_raw, ssm_conv_w[l], ssm_conv_b[l].reshape(1, -1), _pad_lanes(ssm_dt_bias[l]),
            _pad_lanes(ssm_A_log[l]), jnp.repeat(ssm_D[l], SSM_HEADDIM).reshape(1, -1),
            (w_out[l], w_gate[l], w_up[l], w_down[l]), batch, seq, heads, tm=256)
        h1 = _out_proj(y, proj, ssm_norm_g[l].reshape(1, -1), (d_ssm + d_xbc) // d_conv, sc_conv_w[l], wo_b, h,
                       seq, tm=512, tn=512)
        last = l == depth - 1
        h = _ffn(h1, norm_ffn_g[l].reshape(1, -1), wg_b, wu_b, wd_b, norm_final_g.reshape(1, -1),
                 final_norm=last, tm=1024, tf=512)
    return h.reshape(batch, seq, d_model)
```
